```python
import functools
import jax
import jax.numpy as jnp
from jax import lax
import numpy as np

D_MODEL = 4096
BATCH = 4
SEQ = 2048
DEPTH = 1
DEC_BATCH = 128
DEC_SEQ = 4
PAST_LEN = 16384
PAGE_SIZE = 128

MLA_HEADS = 16
MLA_NOPE = 128
MLA_ROPE = 64
MLA_V = 128
Q_LORA = 1024
KV_LORA = 512
MLA_WIDTH = MLA_HEADS * MLA_V
DSA_HEADS = 16
DSA_HEAD_DIM = 128
DSA_WIDTH = DSA_HEADS * DSA_HEAD_DIM
IDX_HEADS = 32
IDX_DIM = 128
TOPK_MAX = 256
MIX_WIDTH = MLA_WIDTH + DSA_WIDTH
ROPE_THETA = 10000.0
EPS = 1e-6
Q_BLOCK = 128
SPLITS = (Q_LORA, KV_LORA + MLA_ROPE, MLA_WIDTH,
          DSA_WIDTH, DSA_HEAD_DIM, DSA_HEAD_DIM, DSA_WIDTH,
          IDX_HEADS * IDX_DIM, IDX_DIM, IDX_HEADS)
IN_WIDTH = sum(SPLITS)

kernel_name = 'hymba_mla_dsa_adaln_step'


def _split_points():
    return [int(v) for v in np.cumsum(SPLITS)[:-1]]


def rms_norm(x, g):
    x32 = x.astype(jnp.float32)
    y = x32 * lax.rsqrt(jnp.mean(x32 * x32, axis=-1, keepdims=True) + EPS)
    return (y * g.astype(jnp.float32)).astype(x.dtype)


def rope(x, pos):
    half = x.shape[-1] // 2
    inv = ROPE_THETA ** (-jnp.arange(half, dtype=jnp.float32) / half)
    ang = pos.astype(jnp.float32)[:, None] * inv[None, :]
    cos = jnp.cos(ang)[:, None, :]
    sin = jnp.sin(ang)[:, None, :]
    x32 = x.astype(jnp.float32)
    x1, x2 = x32[..., :half], x32[..., half:]
    return jnp.concatenate([x1 * cos - x2 * sin, x2 * cos + x1 * sin], axis=-1).astype(x.dtype)


def modulate_in(x, c, g_pre, w_ada, b_ada):
    ada = jax.nn.silu(c) @ w_ada + b_ada
    shift, scale, gate = jnp.split(ada, 3, axis=-1)
    h = rms_norm(x, g_pre) * (1 + scale[:, None, :]) + shift[:, None, :]
    return h, gate


def finish(x, gate, o_mla, z_mla, o_dsa, z_dsa, w_out, g_post):
    o = jnp.concatenate([o_mla * jax.nn.silu(z_mla), o_dsa * jax.nn.silu(z_dsa)], axis=-1) @ w_out
    return x + gate[:, None, :] * rms_norm(o, g_post)


def project(h, pos, w_in, g_q_lora, w_uq, g_kv_lora):
    B, T, _ = h.shape
    u = h @ w_in
    cq, ckv, z_mla, q_d, k_d, v_d, z_dsa, q_i, k_i, w_i = jnp.split(u, _split_points(), axis=-1)
    q = (rms_norm(cq, g_q_lora) @ w_uq).reshape(B, T, MLA_HEADS, MLA_NOPE + MLA_ROPE)
    q_nope = q[..., :MLA_NOPE]
    q_rope = rope(q[..., MLA_NOPE:], pos)
    mla_row = jnp.concatenate([rms_norm(ckv[..., :KV_LORA], g_kv_lora),
                               rope(ckv[..., None, KV_LORA:], pos)[..., 0, :]], axis=-1)
    q_dsa = rope(q_d.reshape(B, T, DSA_HEADS, DSA_HEAD_DIM), pos)
    dsa_row = jnp.concatenate([rope(k_d[..., None, :], pos)[..., 0, :], v_d], axis=-1)
    q_idx = rope(q_i.reshape(B, T, IDX_HEADS, IDX_DIM), pos)
    idx_row = rope(k_i[..., None, :], pos)[..., 0, :]
    w_idx = w_i * IDX_HEADS ** -0.5
    return q_nope, q_rope, mla_row, z_mla, q_dsa, dsa_row, z_dsa, q_idx, idx_row, w_idx


def dsa_attend(q, q_idx, w_idx, q_pos, k_all, v_all, k_idx_all, topk):
    L = k_all.shape[0]
    visible = jnp.arange(L)[None, :] <= q_pos[:, None]
    dots = jnp.einsum('thd,sd->tsh', q_idx, k_idx_all, preferred_element_type=jnp.float32) * IDX_DIM ** -0.5
    score = jnp.einsum('tsh,th->ts', jax.nn.relu(dots), w_idx.astype(jnp.float32))
    score = jnp.where(visible, score, -jnp.inf)
    sel_score, sel = lax.top_k(score, topk)
    valid = jnp.isfinite(sel_score)
    k_sel = k_all[sel]
    v_sel = v_all[sel]
    logits = jnp.einsum('thd,tkd->thk', q, k_sel, preferred_element_type=jnp.float32) * DSA_HEAD_DIM ** -0.5
    p = jax.nn.softmax(jnp.where(valid[:, None, :], logits, -jnp.inf), axis=-1)
    return jnp.einsum('thk,tkd->thd', p.astype(v_sel.dtype), v_sel)


def mla_prompt(q_nope, q_rope, mla_row, w_ukv):
    B, S = q_nope.shape[:2]
    c_kv = mla_row[..., :KV_LORA]
    k_rope = mla_row[..., KV_LORA:]
    kv = (c_kv @ w_ukv).reshape(B, S, MLA_HEADS, MLA_NOPE + MLA_V)
    k_nope = kv[..., :MLA_NOPE]
    v = kv[..., MLA_NOPE:]
    scale = (MLA_NOPE + MLA_ROPE) ** -0.5
    key_pos = jnp.arange(S)

    def block(i):
        s0 = i * Q_BLOCK
        qn = lax.dynamic_slice_in_dim(q_nope, s0, Q_BLOCK, axis=1)
        qr = lax.dynamic_slice_in_dim(q_rope, s0, Q_BLOCK, axis=1)
        logits = (jnp.einsum('bqhd,bkhd->bhqk', qn, k_nope, preferred_element_type=jnp.float32)
                  + jnp.einsum('bqhr,bkr->bhqk', qr, k_rope, preferred_element_type=jnp.float32)) * scale
        mask = key_pos[None, :] <= (s0 + jnp.arange(Q_BLOCK))[:, None]
        p = jax.nn.softmax(jnp.where(mask, logits, -jnp.inf), axis=-1)
        return jnp.einsum('bhqk,bkhv->bqhv', p.astype(v.dtype), v)

    out = lax.map(block, jnp.arange(S // Q_BLOCK))
    return jnp.swapaxes(out, 0, 1).reshape(B, S, MLA_WIDTH)


def dsa_prompt(q_dsa, q_idx, w_idx, dsa_row, idx_row):
    B, S = q_dsa.shape[:2]
    topk = min(TOPK_MAX, S // 4)
    k_all = dsa_row[..., :DSA_HEAD_DIM]
    v_all = dsa_row[..., DSA_HEAD_DIM:]
    attend = jax.vmap(functools.partial(dsa_attend, topk=topk), in_axes=(0, 0, 0, None, 0, 0, 0))

    def block(i):
        s0 = i * Q_BLOCK
        sl = lambda a: lax.dynamic_slice_in_dim(a, s0, Q_BLOCK, axis=1)
        q_pos = s0 + jnp.arange(Q_BLOCK)
        return attend(sl(q_dsa), sl(q_idx), sl(w_idx), q_pos, k_all, v_all, idx_row)

    out = lax.map(block, jnp.arange(S // Q_BLOCK))
    return jnp.swapaxes(out, 0, 1).reshape(B, S, DSA_WIDTH)


def prompt_layer(x, c, g_pre, g_post, w_ada, b_ada, w_in, g_q_lora, w_uq, g_kv_lora, w_ukv, w_out):
    h, gate = modulate_in(x, c, g_pre, w_ada, b_ada)
    pos = jnp.arange(x.shape[1])
    q_nope, q_rope, mla_row, z_mla, q_dsa, dsa_row, z_dsa, q_idx, idx_row, w_idx = project(
        h, pos, w_in, g_q_lora, w_uq, g_kv_lora)
    o_mla = mla_prompt(q_nope, q_rope, mla_row, w_ukv)
    o_dsa = dsa_prompt(q_dsa, q_idx, w_idx, dsa_row, idx_row)
    y = finish(x, gate, o_mla, z_mla, o_dsa, z_dsa, w_out, g_post)
    return y, mla_row, dsa_row, idx_row


def sample_layer(x, c, layer, cache_mla, cache_dsa_kv, cache_idx_k, page_table,
                 g_pre, g_post, w_ada, b_ada, w_in, g_q_lora, w_uq, g_kv_lora, w_ukv, w_out):
    Bd, T, _ = x.shape
    h, gate = modulate_in(x, c, g_pre, w_ada, b_ada)
    pos = PAST_LEN + jnp.arange(T)
    q_nope, q_rope, mla_row, z_mla, q_dsa, dsa_row, z_dsa, q_idx, idx_row, w_idx = project(
        h, pos, w_in, g_q_lora, w_uq, g_kv_lora)
    w_ukv3 = w_ukv.reshape(KV_LORA, MLA_HEADS, MLA_NOPE + MLA_V)
    q_lat = jnp.einsum('bthn,chn->bthc', q_nope, w_ukv3[..., :MLA_NOPE])
    q_mla = jnp.concatenate([q_lat, q_rope], axis=-1)
    scale = (MLA_NOPE + MLA_ROPE) ** -0.5
    L = PAST_LEN + T
    topk = min(TOPK_MAX, L // 4)
    visible = jnp.arange(L)[None, :] <= pos[:, None]

    def per_seq(args):
        pages, q_m, mla_new, q_d, dsa_new, q_i, idx_new, w_i = args
        mla_all = jnp.concatenate([cache_mla[layer, pages].reshape(PAST_LEN, KV_LORA + MLA_ROPE), mla_new], axis=0)
        dsa_all = jnp.concatenate([cache_dsa_kv[layer, pages].reshape(PAST_LEN, 2 * DSA_HEAD_DIM), dsa_new], axis=0)
        idx_all = jnp.concatenate([cache_idx_k[layer, pages].reshape(PAST_LEN, IDX_DIM), idx_new], axis=0)
        logits = jnp.einsum('thc,sc->ths', q_m, mla_all, preferred_element_type=jnp.float32) * scale
        p = jax.nn.softmax(jnp.where(visible[:, None, :], logits, -jnp.inf), axis=-1)
        o_lat = jnp.einsum('ths,sc->thc', p.astype(mla_all.dtype), mla_all[:, :KV_LORA])
        o_d = dsa_attend(q_d, q_i, w_i, pos, dsa_all[:, :DSA_HEAD_DIM], dsa_all[:, DSA_HEAD_DIM:], idx_all, topk)
        return o_lat, o_d

    o_lat, o_d = lax.map(per_seq, (page_table, q_mla, mla_row, q_dsa, dsa_row, q_idx, idx_row, w_idx))
    o_mla = jnp.einsum('bthc,chv->bthv', o_lat, w_ukv3[..., MLA_NOPE:]).reshape(Bd, T, MLA_WIDTH)
    o_dsa = o_d.reshape(Bd, T, DSA_WIDTH)
    y = finish(x, gate, o_mla, z_mla, o_dsa, z_dsa, w_out, g_post)
    return y, mla_row, dsa_row, idx_row


def setup_inputs(seed: int = 0) -> dict:
    key = jax.random.key(seed)
    ks = jax.random.split(key, 20)
    f32 = jnp.float32
    n_pages = PAST_LEN // PAGE_SIZE
    n_used = DEC_BATCH * n_pages
    n_pool = n_used + max(1, n_used // 4)

    def nrm(k, shape, s):
        return jax.random.normal(k, shape, f32) * s

    page_table = jax.random.permutation(ks[5], n_pool)[:n_used].reshape(DEC_BATCH, n_pages).astype(jnp.int32)
    return {
        'x_prompt': nrm(ks[0], (BATCH, SEQ, D_MODEL), 1.0),
        'x_sample': nrm(ks[1], (DEC_BATCH, DEC_SEQ, D_MODEL), 1.0),
        'cache_mla': nrm(ks[2], (DEPTH, n_pool, PAGE_SIZE, KV_LORA + MLA_ROPE), 1.0),
        'cache_dsa_kv': nrm(ks[3], (DEPTH, n_pool, PAGE_SIZE, 2 * DSA_HEAD_DIM), 1.0),
        'cache_idx_k': nrm(ks[4], (DEPTH, n_pool, PAGE_SIZE, IDX_DIM), 1.0),
        'page_table': page_table,
        'c_prompt': nrm(ks[6], (BATCH, D_MODEL), 1.0),
        'c_sample': nrm(ks[7], (DEC_BATCH, D_MODEL), 1.0),
        'g_pre': 1.0 + nrm(ks[8], (DEPTH, D_MODEL), 0.05),
        'g_post': 1.0 + nrm(ks[9], (DEPTH, D_MODEL), 0.05),
        'w_ada': nrm(ks[10], (DEPTH, D_MODEL, 3 * D_MODEL), 0.2 * D_MODEL ** -0.5),
        'b_ada': nrm(ks[11], (DEPTH, 3 * D_MODEL), 0.01),
        'w_in': nrm(ks[12], (DEPTH, D_MODEL, IN_WIDTH), D_MODEL ** -0.5),
        'g_q_lora': 1.0 + nrm(ks[13], (DEPTH, Q_LORA), 0.05),
        'w_uq': nrm(ks[14], (DEPTH, Q_LORA, MLA_HEADS * (MLA_NOPE + MLA_ROPE)), Q_LORA ** -0.5),
        'g_kv_lora': 1.0 + nrm(ks[15], (DEPTH, KV_LORA), 0.05),
        'w_ukv': nrm(ks[16], (DEPTH, KV_LORA, MLA_HEADS * (MLA_NOPE + MLA_V)), KV_LORA ** -0.5),
        'w_out': nrm(ks[17], (DEPTH, MIX_WIDTH, D_MODEL), MIX_WIDTH ** -0.5),
    }


def reference(x_prompt, x_sample, cache_mla, cache_dsa_kv, cache_idx_k, page_table, c_prompt, c_sample,
              g_pre, g_post, w_ada, b_ada, w_in, g_q_lora, w_uq, g_kv_lora, w_ukv, w_out):
    yp, ys = x_prompt, x_sample
    mla_p, dsa_p, idx_p, mla_s, dsa_s, idx_s = [], [], [], [], [], []
    for l in range(DEPTH):
        yp, r_mla, r_dsa, r_idx = prompt_layer(yp, c_prompt, g_pre[l], g_post[l], w_ada[l], b_ada[l], w_in[l],
                                              g_q_lora[l], w_uq[l], g_kv_lora[l], w_ukv[l], w_out[l])
        mla_p.append(r_mla)
        dsa_p.append(r_dsa)
        idx_p.append(r_idx)
        ys, s_mla, s_dsa, s_idx = sample_layer(ys, c_sample, l, cache_mla, cache_dsa_kv, cache_idx_k, page_table,
                                               g_pre[l], g_post[l], w_ada[l], b_ada[l], w_in[l],
                                               g_q_lora[l], w_uq[l], g_kv_lora[l], w_ukv[l], w_out[l])
        mla_s.append(s_mla)
        dsa_s.append(s_dsa)
        idx_s.append(s_idx)
    new_mla_p = jnp.stack(mla_p)
    new_dsa_kv_p = jnp.stack(dsa_p)
    new_idx_k_p = jnp.stack(idx_p)
    new_mla_s = jnp.stack(mla_s)
    new_dsa_kv_s = jnp.stack(dsa_s)
    new_idx_k_s = jnp.stack(idx_s)
    return (yp, ys, new_mla_p, new_dsa_kv_p, new_idx_k_p, new_mla_s, new_dsa_kv_s, new_idx_k_s)
```

```python
import functools

import jax
import jax.numpy as jnp
from jax import lax
from jax.experimental import pallas as pl
from jax.experimental.pallas import tpu as pltpu

MLA_HEADS = 16
MLA_NOPE = 128
MLA_ROPE = 64
MLA_V = 128
Q_LORA = 1024
KV_LORA = 512
DSA_HEADS = 16
DSA_HEAD_DIM = 128
IDX_HEADS = 32
IDX_DIM = 128
TOPK_MAX = 256
ROPE_THETA = 10000.0
EPS = 1e-6

MLA_WIDTH = MLA_HEADS * MLA_V
DSA_WIDTH = DSA_HEADS * DSA_HEAD_DIM
MIX_WIDTH = MLA_WIDTH + DSA_WIDTH
MLA_ROW = KV_LORA + MLA_ROPE
MLA_QK = MLA_NOPE + MLA_ROPE
MLA_SCALE = MLA_QK ** -0.5
DSA_SCALE = DSA_HEAD_DIM ** -0.5
IDX_SCALE = IDX_DIM ** -0.5
IDX_W_SCALE = IDX_HEADS ** -0.5

LANES = 128
IN_TILE = 1024
N_IN_TILES = 12
W_IDX_LANE0 = MLA_ROPE
VMEM_LIMIT = 56 * 1024 * 1024

F32 = jnp.float32
BF16 = jnp.bfloat16
NEG_INF = float("-inf")


def _cparams(n_axes):
    return pltpu.CompilerParams(dimension_semantics=("arbitrary",) * n_axes, vmem_limit_bytes=VMEM_LIMIT)


def _silu(x):
    return x / (1.0 + jnp.exp(-x))


def _rms(x, g):
    return x * lax.rsqrt(jnp.mean(x * x, axis=-1, keepdims=True) + EPS) * g


def _lane_iota(shape):
    return lax.broadcasted_iota(jnp.int32, shape, len(shape) - 1)


def _rope_full(a, cos, sin):
    return a * cos + pltpu.roll(a, LANES // 2, 1) * sin


def _rope_pair(a, cos, sin):
    lane = _lane_iota(a.shape)
    fwd = pltpu.roll(a, LANES - MLA_ROPE // 2, 1)
    bwd = pltpu.roll(a, MLA_ROPE // 2, 1)
    partner = jnp.where((lane % MLA_ROPE) < MLA_ROPE // 2, fwd, bwd)
    return a * cos + partner * sin


def _rope_tables(pos, half):
    inv = ROPE_THETA ** (-jnp.arange(half, dtype=F32) / half)
    ang = pos.astype(F32)[:, None] * inv[None, :]
    cos, sin = jnp.cos(ang), jnp.sin(ang)
    reps = LANES // (2 * half)
    return (jnp.tile(jnp.concatenate([cos, cos], -1), (1, reps)),
            jnp.tile(jnp.concatenate([-sin, sin], -1), (1, reps)))


def _ada_kernel(c_ref, w_ref, b_ref, o_ref):
    s = _silu(c_ref[...]).astype(BF16)
    o_ref[...] = jnp.dot(s, w_ref[...].astype(BF16), preferred_element_type=F32) + b_ref[...]


def _ada(c_all, w_ada, b_ada):
    rows, d = c_all.shape
    n = w_ada.shape[1]
    tn = 512
    return pl.pallas_call(
        _ada_kernel,
        out_shape=jax.ShapeDtypeStruct((rows, n), F32),
        grid=(n // tn,),
        in_specs=[pl.BlockSpec((rows, d), lambda j: (0, 0)),
                  pl.BlockSpec((d, tn), lambda j: (0, j)),
                  pl.BlockSpec((1, tn), lambda j: (0, j))],
        out_specs=pl.BlockSpec((rows, tn), lambda j: (0, j)),
        compiler_params=_cparams(1),
        name="ada",
    )(c_all, w_ada, b_ada)


def _prenorm_kernel(x_ref, g_ref, sc_ref, sh_ref, h_ref):
    h_ref[...] = (_rms(x_ref[...], g_ref[...]) * (1.0 + sc_ref[...]) + sh_ref[...]).astype(BF16)


def _prenorm(x2, g_pre, scale3, shift3, tr, group_of_tile):
    m, d = x2.shape
    r = scale3.shape[1]
    mod_spec = pl.BlockSpec((None, r, d), lambda i: (group_of_tile(i), 0, 0))
    return pl.pallas_call(
        _prenorm_kernel,
        out_shape=jax.ShapeDtypeStruct((m, d), BF16),
        grid=(m // tr,),
        in_specs=[pl.BlockSpec((tr, d), lambda i: (i, 0)),
                  pl.BlockSpec((1, d), lambda i: (0, 0)),
                  mod_spec, mod_spec],
        out_specs=pl.BlockSpec((tr, d), lambda i: (i, 0)),
        compiler_params=_cparams(1),
        name="prenorm",
    )(x2, g_pre, scale3, shift3)


def _permute_w_in(w_in):
    o_cq = 0
    o_ckv = o_cq + Q_LORA
    o_kr = o_ckv + KV_LORA
    o_zm = o_kr + MLA_ROPE
    o_qd = o_zm + MLA_WIDTH
    o_kd = o_qd + DSA_WIDTH
    o_vd = o_kd + DSA_HEAD_DIM
    o_zd = o_vd + DSA_HEAD_DIM
    o_qi = o_zd + DSA_WIDTH
    o_ki = o_qi + IDX_HEADS * IDX_DIM
    o_wi = o_ki + IDX_DIM
    end = o_wi + IDX_HEADS
    assert end == w_in.shape[1]
    seg = lambda a, b: w_in[:, a:b]
    pad = jnp.zeros((w_in.shape[0], LANES - MLA_ROPE - IDX_HEADS), w_in.dtype)
    cols = [seg(o_cq, o_ckv), seg(o_zm, o_qd), seg(o_qd, o_kd), seg(o_zd, o_qi), seg(o_qi, o_ki),
            seg(o_ckv, o_kr), seg(o_kd, o_vd), seg(o_vd, o_zd), seg(o_ki, o_wi), seg(o_kr, o_zm),
            seg(o_wi, end), pad]
    w = jnp.concatenate(cols, axis=1).astype(BF16)
    assert w.shape[1] == N_IN_TILES * IN_TILE
    return w


def _inproj_kernel(h_ref, w_ref, c128_ref, s128_ref, c64_ref, s64_ref, gq_ref, gkv_ref,
                   cq_ref, sz_ref, qd_ref, qi_ref, mla_ref, dsa_ref, idx_ref, misc_ref, *, head_major):
    j = pl.program_id(1)
    acc = jnp.dot(h_ref[...], w_ref[...], preferred_element_type=F32)
    heads_per_tile = IN_TILE // LANES

    def store_heads(ref):
        cos, sin = c128_ref[...], s128_ref[...]
        for hh in range(heads_per_tile):
            r = _rope_full(acc[:, hh * LANES:(hh + 1) * LANES], cos, sin).astype(BF16)
            if head_major:
                ref[hh] = r
            else:
                ref[:, hh * LANES:(hh + 1) * LANES] = r

    @pl.when(j == 0)
    def _():
        cq_ref[...] = _rms(acc, gq_ref[...]).astype(BF16)

    @pl.when((j == 1) | (j == 2) | (j == 5) | (j == 6))
    def _():
        sz_ref[...] = _silu(acc).astype(BF16)

    @pl.when((j == 3) | (j == 4))
    def _():
        store_heads(qd_ref)

    @pl.when((j >= 7) & (j <= 10))
    def _():
        store_heads(qi_ref)

    @pl.when(j == N_IN_TILES - 1)
    def _():
        cos, sin = c128_ref[...], s128_ref[...]
        o_kd = KV_LORA
        o_vd = o_kd + DSA_HEAD_DIM
        o_ki = o_vd + DSA_HEAD_DIM
        o_misc = o_ki + IDX_DIM
        g = acc[:, o_misc:o_misc + LANES]
        lane = _lane_iota(g.shape)
        misc = jnp.where(lane < MLA_ROPE, _rope_pair(g, c64_ref[...], s64_ref[...]), g * IDX_W_SCALE)
        mla_ref[:, 0:KV_LORA] = _rms(acc[:, 0:KV_LORA], gkv_ref[...])
        mla_ref[:, KV_LORA:MLA_ROW] = misc[:, 0:MLA_ROPE]
        dsa_ref[:, 0:DSA_HEAD_DIM] = _rope_full(acc[:, o_kd:o_vd], cos, sin)
        dsa_ref[:, DSA_HEAD_DIM:2 * DSA_HEAD_DIM] = acc[:, o_vd:o_ki]
        idx_ref[...] = _rope_full(acc[:, o_ki:o_misc], cos, sin)
        misc_ref[...] = misc


def _inproj(h, w_perm, tabs, g_q, g_kv, tm, tiles_per_seq, head_major):
    m, d = h.shape
    c128, s128, c64, s64 = tabs
    hpt = IN_TILE // LANES
    ge = lambda j, k: (j >= k).astype(jnp.int32)
    tab_spec = pl.BlockSpec((tm, LANES), lambda i, j: (i % tiles_per_seq, 0))
    row_spec = lambda w: pl.BlockSpec((tm, w), lambda i, j: (i, 0))
    if head_major:
        qd_shape, qi_shape = (DSA_HEADS, m, LANES), (IDX_HEADS, m, LANES)
        qd_spec = pl.BlockSpec((hpt, tm, LANES), lambda i, j: (ge(j, 4), i, 0))
        qi_spec = pl.BlockSpec((hpt, tm, LANES), lambda i, j: (jnp.clip(j - 7, 0, 3), i, 0))
    else:
        qd_shape, qi_shape = (m, DSA_WIDTH), (m, IDX_HEADS * IDX_DIM)
        qd_spec = pl.BlockSpec((tm, IN_TILE), lambda i, j: (i, ge(j, 4)))
        qi_spec = pl.BlockSpec((tm, IN_TILE), lambda i, j: (i, jnp.clip(j - 7, 0, 3)))
    out_shape = (jax.ShapeDtypeStruct((m, Q_LORA), BF16),
                 jax.ShapeDtypeStruct((m, MIX_WIDTH), BF16),
                 jax.ShapeDtypeStruct(qd_shape, BF16),
                 jax.ShapeDtypeStruct(qi_shape, BF16),
                 jax.ShapeDtypeStruct((m, MLA_ROW), F32),
                 jax.ShapeDtypeStruct((m, 2 * DSA_HEAD_DIM), F32),
                 jax.ShapeDtypeStruct((m, IDX_DIM), F32),
                 jax.ShapeDtypeStruct((m, LANES), F32))
    out_specs = (row_spec(Q_LORA),
                 pl.BlockSpec((tm, IN_TILE), lambda i, j: (i, ge(j, 2) + ge(j, 5) + ge(j, 6))),
                 qd_spec, qi_spec,
                 row_spec(MLA_ROW), row_spec(2 * DSA_HEAD_DIM), row_spec(IDX_DIM), row_spec(LANES))
    return pl.pallas_call(
        functools.partial(_inproj_kernel, head_major=head_major),
        out_shape=out_shape,
        grid=(m // tm, N_IN_TILES),
        in_specs=[pl.BlockSpec((tm, d), lambda i, j: (i, 0)),
                  pl.BlockSpec((d, IN_TILE), lambda i, j: (0, j)),
                  tab_spec, tab_spec, tab_spec, tab_spec,
                  pl.BlockSpec((1, Q_LORA), lambda i, j: (0, 0)),
                  pl.BlockSpec((1, KV_LORA), lambda i, j: (0, 0))],
        out_specs=out_specs,
        compiler_params=_cparams(2),
        name="inproj",
    )(h, w_perm, c128, s128, c64, s64, g_q, g_kv)


def _permute_w_uq(w_uq):
    w3 = w_uq.reshape(Q_LORA, MLA_HEADS, MLA_QK)
    nope = w3[:, :, :MLA_NOPE].reshape(Q_LORA, MLA_HEADS * MLA_NOPE)
    rope = w3[:, :, MLA_NOPE:].reshape(Q_LORA, MLA_HEADS * MLA_ROPE)
    return jnp.concatenate([nope, rope], axis=1).astype(BF16)


def _uq_kernel(cq_ref, w_ref, c64_ref, s64_ref, *out_refs, head_major):
    acc = jnp.dot(cq_ref[...], w_ref[...], preferred_element_type=F32)
    cos, sin = c64_ref[...], s64_ref[...]
    rope0 = MLA_HEADS * MLA_NOPE
    if head_major:
        (q_ref,) = out_refs
        for h in range(MLA_HEADS):
            q_ref[h, :, 0:MLA_NOPE] = acc[:, h * MLA_NOPE:(h + 1) * MLA_NOPE].astype(BF16)
        for p in range(MLA_HEADS // 2):
            r = _rope_pair(acc[:, rope0 + p * LANES:rope0 + (p + 1) * LANES], cos, sin)
            lane = _lane_iota(r.shape)
            q_ref[2 * p, :, MLA_NOPE:2 * MLA_NOPE] = jnp.where(lane < MLA_ROPE, r, 0.0).astype(BF16)
            q_ref[2 * p + 1, :, MLA_NOPE:2 * MLA_NOPE] = jnp.where(
                lane < MLA_ROPE, pltpu.roll(r, MLA_ROPE, 1), 0.0).astype(BF16)
    else:
        qn_ref, qr_ref = out_refs
        qn_ref[...] = acc[:, 0:rope0].astype(BF16)
        for p in range(MLA_HEADS // 2):
            r = _rope_pair(acc[:, rope0 + p * LANES:rope0 + (p + 1) * LANES], cos, sin)
            qr_ref[:, p * LANES:(p + 1) * LANES] = r.astype(BF16)


def _uq(cq, w_uq_perm, c64, s64, tm, tiles_per_seq, head_major):
    m = cq.shape[0]
    n = w_uq_perm.shape[1]
    if head_major:
        out_shape = jax.ShapeDtypeStruct((MLA_HEADS, m, 2 * MLA_NOPE), BF16)
        out_specs = pl.BlockSpec((MLA_HEADS, tm, 2 * MLA_NOPE), lambda i: (0, i, 0))
    else:
        out_shape = (jax.ShapeDtypeStruct((m, MLA_HEADS * MLA_NOPE), BF16),
                     jax.ShapeDtypeStruct((m, MLA_HEADS * MLA_ROPE), BF16))
        out_specs = (pl.BlockSpec((tm, MLA_HEADS * MLA_NOPE), lambda i: (i, 0)),
                     pl.BlockSpec((tm, MLA_HEADS * MLA_ROPE), lambda i: (i, 0)))
    tab_spec = pl.BlockSpec((tm, LANES), lambda i: (i % tiles_per_seq, 0))
    return pl.pallas_call(
        functools.partial(_uq_kernel, head_major=head_major),
        out_shape=out_shape,
        grid=(m // tm,),
        in_specs=[pl.BlockSpec((tm, Q_LORA), lambda i: (i, 0)),
                  pl.BlockSpec((Q_LORA, n), lambda i: (0, 0)),
                  tab_spec, tab_spec],
        out_specs=out_specs,
        compiler_params=_cparams(1),
        name="uq",
    )(cq, w_uq_perm, c64, s64)


def _kvup_kernel(mla_ref, misc_ref, w_ref, k_ref, v_ref):
    c = mla_ref[:, 0:KV_LORA].astype(BF16)
    kv = jnp.dot(c, w_ref[...], preferred_element_type=F32)
    misc = misc_ref[...]
    krz = jnp.where(_lane_iota(misc.shape) < MLA_ROPE, misc, 0.0).astype(BF16)
    per_head = MLA_NOPE + MLA_V
    for h in range(MLA_HEADS):
        k_ref[h, :, 0:MLA_NOPE] = kv[:, h * per_head:h * per_head + MLA_NOPE].astype(BF16)
        k_ref[h, :, MLA_NOPE:2 * MLA_NOPE] = krz
        v_ref[h] = kv[:, h * per_head + MLA_NOPE:(h + 1) * per_head].astype(BF16)


def _kvup(mla_row, misc, w_ukv_bf, tm):
    m = mla_row.shape[0]
    n = w_ukv_bf.shape[1]
    return pl.pallas_call(
        _kvup_kernel,
        out_shape=(jax.ShapeDtypeStruct((MLA_HEADS, m, 2 * MLA_NOPE), BF16),
                   jax.ShapeDtypeStruct((MLA_HEADS, m, MLA_V), BF16)),
        grid=(m // tm,),
        in_specs=[pl.BlockSpec((tm, MLA_ROW), lambda i: (i, 0)),
                  pl.BlockSpec((tm, LANES), lambda i: (i, 0)),
                  pl.BlockSpec((KV_LORA, n), lambda i: (0, 0))],
        out_specs=(pl.BlockSpec((MLA_HEADS, tm, 2 * MLA_NOPE), lambda i: (0, i, 0)),
                   pl.BlockSpec((MLA_HEADS, tm, MLA_V), lambda i: (0, i, 0))),
        compiler_params=_cparams(1),
        name="kvup",
    )(mla_row, misc, w_ukv_bf)


def _online_softmax_step(s, v, m_scr, l_scr, acc_scr):
    m_prev = m_scr[...]
    m_new = jnp.maximum(m_prev, jnp.max(s, axis=-1, keepdims=True))
    m_safe = jnp.where(m_new == NEG_INF, 0.0, m_new)
    alpha = jnp.exp(m_prev - m_safe)
    p = jnp.exp(s - m_safe)
    l_scr[...] = alpha * l_scr[...] + jnp.sum(p, axis=-1, keepdims=True)
    acc_scr[...] = alpha * acc_scr[...] + jnp.dot(p.astype(BF16), v, preferred_element_type=F32)
    m_scr[...] = m_new


def _nt_dot(a, b):
    return lax.dot_general(a, b, (((1,), (1,)), ((), ())), preferred_element_type=F32)


def _mla_prompt_kernel(q_ref, k_ref, v_ref, o_ref, m_scr, l_scr, acc_scr, *, tq):
    qi = pl.program_id(2)
    heads = q_ref.shape[0]

    def head(h, carry):
        q = q_ref[h]
        m_scr[...] = jnp.full(m_scr.shape, NEG_INF, F32)
        l_scr[...] = jnp.zeros(l_scr.shape, F32)
        acc_scr[...] = jnp.zeros(acc_scr.shape, F32)

        def chunk(c, diagonal):
            start = pl.multiple_of(c * tq, tq)
            s = _nt_dot(q, k_ref[h, pl.ds(start, tq), :]) * MLA_SCALE
            if diagonal:
                row = lax.broadcasted_iota(jnp.int32, s.shape, 0)
                col = lax.broadcasted_iota(jnp.int32, s.shape, 1)
                s = jnp.where(col <= row, s, NEG_INF)
            _online_softmax_step(s, v_ref[h, pl.ds(start, tq), :], m_scr, l_scr, acc_scr)

        def below(c, carry2):
            chunk(c, False)
            return carry2

        lax.fori_loop(0, qi, below, 0)
        chunk(qi, True)
        o_ref[h] = (acc_scr[...] / l_scr[...]).astype(BF16)
        return carry

    lax.fori_loop(0, heads, head, 0)


def _mla_prompt(q_hm, k_hm, v_hm, batch, seq, tq, head_groups):
    heads, m, dq = q_hm.shape
    hg = heads // head_groups
    nq = seq // tq
    return pl.pallas_call(
        functools.partial(_mla_prompt_kernel, tq=tq),
        out_shape=jax.ShapeDtypeStruct((heads, m, MLA_V), BF16),
        grid=(batch, head_groups, nq),
        in_specs=[pl.BlockSpec((hg, tq, dq), lambda b, g, i: (g, b * nq + i, 0)),
                  pl.BlockSpec((hg, seq, dq), lambda b, g, i: (g, b, 0)),
                  pl.BlockSpec((hg, seq, MLA_V), lambda b, g, i: (g, b, 0))],
        out_specs=pl.BlockSpec((hg, tq, MLA_V), lambda b, g, i: (g, b * nq + i, 0)),
        scratch_shapes=[pltpu.VMEM((tq, 1), F32), pltpu.VMEM((tq, 1), F32), pltpu.VMEM((tq, MLA_V), F32)],
        compiler_params=_cparams(3),
        name="mla_prompt",
    )(q_hm, k_hm, v_hm)


INT_MIN = -2 ** 31
KEY_NEG_INF = INT_MIN + 0x7FFFFF
RANK_CHUNK = 256


def _sort_key(x):
    x = jnp.where(x == 0.0, 0.0, x)
    b = lax.bitcast_convert_type(x, jnp.int32)
    return b ^ ((b >> 31) & 0x7FFFFFFF)


def _count(mask, axes):
    return jnp.sum(jnp.where(mask, 1.0, 0.0), axis=axes, keepdims=True)


def _kth_largest_key(key, k, axes):
    shape = tuple(1 if a in axes else s for a, s in enumerate(key.shape))

    def body(it, prefix):
        cand_u = prefix | jnp.left_shift(jnp.int32(1), 31 - it)
        cnt = _count(key >= (cand_u ^ INT_MIN), axes)
        return jnp.where(cnt >= k, cand_u, prefix)

    return lax.fori_loop(0, 32, body, jnp.zeros(shape, jnp.int32)) ^ INT_MIN


def _dsa_prompt_kernel(qi_ref, kidx_ref, misc_ref, qd_ref, kv_ref, o_ref,
                       score_scr, bias_scr, wt_scr, kidx_scr, kd_scr, vt_scr, *, tq, topk):
    i = pl.program_id(1)
    seq = kidx_ref.shape[0]
    kidx_scr[...] = kidx_ref[...].astype(BF16)
    kd_scr[...] = kv_ref[:, 0:DSA_HEAD_DIM].astype(BF16)
    vt_scr[...] = kv_ref[:, DSA_HEAD_DIM:2 * DSA_HEAD_DIM].T.astype(BF16)
    wt_scr[...] = misc_ref[...].T * IDX_SCALE
    score_scr[...] = jnp.zeros(score_scr.shape, F32)

    def idx_head(h, carry):
        d = _nt_dot(kidx_scr[...], qi_ref[h])
        score_scr[...] += jnp.maximum(d, 0.0) * wt_scr[pl.ds(W_IDX_LANE0 + h, 1), :]
        return carry

    lax.fori_loop(0, IDX_HEADS, idx_head, 0)

    key_pos = lax.broadcasted_iota(jnp.int32, (seq, tq), 0)
    q_pos = i * tq + lax.broadcasted_iota(jnp.int32, (seq, tq), 1)
    key = _sort_key(jnp.where(key_pos <= q_pos, score_scr[...], NEG_INF))
    thr = _kth_largest_key(key, topk, (0,))
    n_ge = _count(key >= thr, (0,))
    has_ties = jnp.max(jnp.where((n_ge > topk) & (thr > KEY_NEG_INF), 1.0, 0.0))

    @pl.when(has_ties == 0.0)
    def _():
        bias_scr[...] = jnp.where((key >= thr) & (key > KEY_NEG_INF), 0.0, NEG_INF)

    @pl.when(has_ties > 0.0)
    def _():
        need = topk - _count(key > thr, (0,))
        r = lax.broadcasted_iota(jnp.int32, (RANK_CHUNK, RANK_CHUNK), 0)
        c = lax.broadcasted_iota(jnp.int32, (RANK_CHUNK, RANK_CHUNK), 1)
        tri = jnp.where(c <= r, 1.0, 0.0).astype(BF16)
        before = jnp.zeros((1, tq), F32)
        for ck in range(seq // RANK_CHUNK):
            rows = slice(ck * RANK_CHUNK, (ck + 1) * RANK_CHUNK)
            kc = key[rows]
            tie = kc == thr
            rank = before + jnp.dot(tri, jnp.where(tie, 1.0, 0.0).astype(BF16), preferred_element_type=F32)
            before = rank[RANK_CHUNK - 1:RANK_CHUNK, :]
            sel = ((kc > thr) | (tie & (rank <= need))) & (kc > KEY_NEG_INF)
            bias_scr[rows, :] = jnp.where(sel, 0.0, NEG_INF)

    def attn_head(h, carry):
        lg = _nt_dot(kd_scr[...], qd_ref[h]) * DSA_SCALE + bias_scr[...]
        e = jnp.exp(lg - jnp.max(lg, axis=0, keepdims=True))
        l = jnp.sum(e, axis=0, keepdims=True)
        ot = jnp.dot(vt_scr[...], e.astype(BF16), preferred_element_type=F32)
        o_ref[h] = (ot / l).T.astype(BF16)
        return carry

    lax.fori_loop(0, DSA_HEADS, attn_head, 0)


def _dsa_prompt(qi_hm, idx_row, misc, qd_hm, dsa_row, batch, seq, tq):
    m = idx_row.shape[0]
    nq = seq // tq
    topk = min(TOPK_MAX, seq // 4)
    row = lambda b, i: (b * nq + i, 0)
    return pl.pallas_call(
        functools.partial(_dsa_prompt_kernel, tq=tq, topk=topk),
        out_shape=jax.ShapeDtypeStruct((DSA_HEADS, m, DSA_HEAD_DIM), BF16),
        grid=(batch, nq),
        in_specs=[pl.BlockSpec((IDX_HEADS, tq, IDX_DIM), lambda b, i: (0, b * nq + i, 0)),
                  pl.BlockSpec((seq, IDX_DIM), lambda b, i: (b, 0)),
                  pl.BlockSpec((tq, LANES), row),
                  pl.BlockSpec((DSA_HEADS, tq, DSA_HEAD_DIM), lambda b, i: (0, b * nq + i, 0)),
                  pl.BlockSpec((seq, 2 * DSA_HEAD_DIM), lambda b, i: (b, 0))],
        out_specs=pl.BlockSpec((DSA_HEADS, tq, DSA_HEAD_DIM), lambda b, i: (0, b * nq + i, 0)),
        scratch_shapes=[pltpu.VMEM((seq, tq), F32), pltpu.VMEM((seq, tq), F32),
                        pltpu.VMEM((LANES, tq), F32), pltpu.VMEM((seq, IDX_DIM), BF16),
                        pltpu.VMEM((seq, DSA_HEAD_DIM), BF16), pltpu.VMEM((DSA_HEAD_DIM, seq), BF16)],
        compiler_params=_cparams(2),
        name="dsa_prompt",
    )(qi_hm, idx_row, misc, qd_hm, dsa_row)


def _outproj_kernel(oa_ref, ob_ref, sz_ref, w_ref, o_ref, lhs_scr, *, head_major):
    @pl.when(pl.program_id(1) == 0)
    def _():
        if head_major:
            for h in range(MLA_HEADS):
                cols = slice(h * MLA_V, (h + 1) * MLA_V)
                lhs_scr[:, cols] = oa_ref[h] * sz_ref[:, cols]
            for h in range(DSA_HEADS):
                cols = slice(MLA_WIDTH + h * DSA_HEAD_DIM, MLA_WIDTH + (h + 1) * DSA_HEAD_DIM)
                lhs_scr[:, cols] = ob_ref[h] * sz_ref[:, cols]
        else:
            lhs_scr[:, 0:MLA_WIDTH] = oa_ref[...] * sz_ref[:, 0:MLA_WIDTH]
            lhs_scr[:, MLA_WIDTH:MIX_WIDTH] = ob_ref[...] * sz_ref[:, MLA_WIDTH:MIX_WIDTH]

    o_ref[...] = jnp.dot(lhs_scr[...], w_ref[...], preferred_element_type=F32)


def _outproj(o_mla, o_dsa, sz, w_out_bf, tm, head_major):
    m = sz.shape[0]
    d = w_out_bf.shape[1]
    tn = 1024
    if head_major:
        oa_spec = pl.BlockSpec((MLA_HEADS, tm, MLA_V), lambda i, j: (0, i, 0))
        ob_spec = pl.BlockSpec((DSA_HEADS, tm, DSA_HEAD_DIM), lambda i, j: (0, i, 0))
    else:
        oa_spec = pl.BlockSpec((tm, MLA_WIDTH), lambda i, j: (i, 0))
        ob_spec = pl.BlockSpec((tm, DSA_WIDTH), lambda i, j: (i, 0))
    return pl.pallas_call(
        functools.partial(_outproj_kernel, head_major=head_major),
        out_shape=jax.ShapeDtypeStruct((m, d), F32),
        grid=(m // tm, d // tn),
        in_specs=[oa_spec, ob_spec,
                  pl.BlockSpec((tm, MIX_WIDTH), lambda i, j: (i, 0)),
                  pl.BlockSpec((MIX_WIDTH, tn), lambda i, j: (0, j))],
        out_specs=pl.BlockSpec((tm, tn), lambda i, j: (i, j)),
        scratch_shapes=[pltpu.VMEM((tm, MIX_WIDTH), BF16)],
        compiler_params=_cparams(2),
        name="outproj",
    )(o_mla, o_dsa, sz, w_out_bf)


def _finish_kernel(x_ref, o_ref, gate_ref, g_ref, y_ref):
    y_ref[...] = x_ref[...] + gate_ref[...] * _rms(o_ref[...], g_ref[...])


def _finish(x2, o_raw, gate3, g_post, tr, group_of_tile):
    m, d = x2.shape
    r = gate3.shape[1]
    row_spec = pl.BlockSpec((tr, d), lambda i: (i, 0))
    return pl.pallas_call(
        _finish_kernel,
        out_shape=jax.ShapeDtypeStruct((m, d), F32),
        grid=(m // tr,),
        in_specs=[row_spec, row_spec,
                  pl.BlockSpec((None, r, d), lambda i: (group_of_tile(i), 0, 0)),
                  pl.BlockSpec((1, d), lambda i: (0, 0))],
        out_specs=row_spec,
        compiler_params=_cparams(1),
        name="finish",
    )(x2, o_raw, gate3, g_post)


Q_MLA_W = KV_LORA + LANES


def _absorb_kernel(qn_ref, qr_ref, w_ref, o_ref):
    h = pl.program_id(0)
    o_ref[:, 0:KV_LORA] = _nt_dot(qn_ref[...], w_ref[...]).astype(BF16)
    g = qr_ref[...]
    g = jnp.where(h % 2 == 0, g, pltpu.roll(g.astype(F32), MLA_ROPE, 1).astype(BF16))
    o_ref[:, KV_LORA:Q_MLA_W] = jnp.where(_lane_iota(g.shape) < MLA_ROPE, g, jnp.zeros_like(g))


def _absorb(q_nope, q_rope, w_ukv_bf):
    m = q_nope.shape[0]
    return pl.pallas_call(
        _absorb_kernel,
        out_shape=jax.ShapeDtypeStruct((m, MLA_HEADS * Q_MLA_W), BF16),
        grid=(MLA_HEADS,),
        in_specs=[pl.BlockSpec((m, MLA_NOPE), lambda h: (0, h)),
                  pl.BlockSpec((m, LANES), lambda h: (0, h // 2)),
                  pl.BlockSpec((KV_LORA, MLA_NOPE), lambda h: (0, 2 * h))],
        out_specs=pl.BlockSpec((m, Q_MLA_W), lambda h: (0, h)),
        compiler_params=_cparams(1),
        name="absorb",
    )(q_nope, q_rope, w_ukv_bf)


def _unabsorb_kernel(ol_ref, w_ref, o_ref):
    o_ref[...] = jnp.dot(ol_ref[...], w_ref[...], preferred_element_type=F32).astype(BF16)


def _unabsorb(o_lat, w_ukv_bf):
    m = o_lat.shape[0]
    return pl.pallas_call(
        _unabsorb_kernel,
        out_shape=jax.ShapeDtypeStruct((m, MLA_WIDTH), BF16),
        grid=(MLA_HEADS,),
        in_specs=[pl.BlockSpec((m, KV_LORA), lambda h: (0, h)),
                  pl.BlockSpec((KV_LORA, MLA_V), lambda h: (0, 2 * h + 1))],
        out_specs=pl.BlockSpec((m, MLA_V), lambda h: (0, h)),
        compiler_params=_cparams(1),
        name="unabsorb",
    )(o_lat, w_ukv_bf)


SCORE_ROWS = 8
MAX_CHUNK_PAGES = 16


def _pages_per_chunk(n_pages):
    cp = MAX_CHUNK_PAGES
    while n_pages % cp:
        cp //= 2
    return cp


def _page_copies(pt_ref, cache_ref, buf, sem, seq, chunk, slot, cp, page, width):
    return [pltpu.make_async_copy(cache_ref.at[pt_ref[seq, chunk * cp + p]],
                                  buf.at[slot, pl.ds(p * page, page), pl.ds(0, width)],
                                  sem.at[slot])
            for p in range(cp)]


def _ring_step(b, ci, nc, copies_for):
    nb = pl.num_programs(0)
    slot = (b * nc + ci) % 2
    last = ci + 1 == nc
    nxt_b = jnp.where(last, b + 1, b)
    nxt_c = jnp.where(last, 0, ci + 1)

    @pl.when(jnp.logical_not(last) | (b + 1 < nb))
    def _():
        for c in copies_for(nxt_b, nxt_c, 1 - slot):
            c.start()

    for c in copies_for(b, ci, slot):
        c.wait()
    return slot


def _expand_rows(x, t_new, reps):
    return jnp.concatenate([jnp.broadcast_to(x[t:t + 1], (reps, x.shape[1])) for t in range(t_new)], axis=0)


def _sidx_kernel(pt_ref, q_ref, wsel_ref, knew_ref, cache_ref, bias_ref,
                 buf, sem, score_scr, tail_scr, *, cp, nc, page, topk, t_new):
    b = pl.program_id(0)
    ck = cp * page
    copies_for = lambda s, c, slot: _page_copies(pt_ref, cache_ref, buf, sem, s, c, slot, cp, page, IDX_DIM)

    @pl.when(b == 0)
    def _():
        for c in copies_for(0, 0, 0):
            c.start()

    q = q_ref[...]
    wsel = wsel_ref[...]

    def scores(keys_bf):
        r = jnp.maximum(_nt_dot(q, keys_bf), 0.0).astype(BF16)
        return jnp.dot(wsel, r, preferred_element_type=F32)[0:SCORE_ROWS] * IDX_SCALE

    def chunk(ci, carry):
        slot = _ring_step(b, ci, nc, copies_for)
        score_scr[ci] = scores(buf[slot].astype(BF16))
        return carry

    lax.fori_loop(0, nc, chunk, 0)

    tail_scr[...] = jnp.zeros(tail_scr.shape, F32)
    tail_scr[0:t_new, :] = knew_ref[...]
    st = scores(tail_scr[...].astype(BF16))
    row = lax.broadcasted_iota(jnp.int32, st.shape, 0)
    col = lax.broadcasted_iota(jnp.int32, st.shape, 1)
    score_scr[nc, :, 0:LANES] = jnp.where((col <= row) & (col < t_new), st, NEG_INF)
    score_scr[nc, :, LANES:ck] = jnp.full((SCORE_ROWS, ck - LANES), NEG_INF, F32)

    key = _sort_key(score_scr[...])
    thr = _kth_largest_key(key, topk, (0, 2))
    n_ge = _count(key >= thr, (0, 2))
    real = lax.broadcasted_iota(jnp.int32, thr.shape, 1) < t_new
    has_ties = jnp.max(jnp.where((n_ge > topk) & (thr > KEY_NEG_INF) & real, 1.0, 0.0))

    @pl.when(has_ties == 0.0)
    def _():
        bias_ref[...] = jnp.where((key >= thr) & (key > KEY_NEG_INF), 0.0, NEG_INF)

    @pl.when(has_ties > 0.0)
    def _():
        thr2 = thr[0]
        need = topk - _count(key > thr, (0, 2))[0]
        r = lax.broadcasted_iota(jnp.int32, (RANK_CHUNK, RANK_CHUNK), 0)
        c = lax.broadcasted_iota(jnp.int32, (RANK_CHUNK, RANK_CHUNK), 1)
        tri = jnp.where(r <= c, 1.0, 0.0)
        before = jnp.zeros((SCORE_ROWS, 1), F32)
        for ci in range(nc + 1):
            for sub in range(ck // RANK_CHUNK):
                cols = slice(sub * RANK_CHUNK, (sub + 1) * RANK_CHUNK)
                kc = key[ci, :, cols]
                tie = kc == thr2
                rank = before + jnp.dot(jnp.where(tie, 1.0, 0.0), tri, preferred_element_type=F32)
                before = rank[:, RANK_CHUNK - 1:RANK_CHUNK]
                sel = ((kc > thr2) | (tie & (rank <= need))) & (kc > KEY_NEG_INF)
                bias_ref[ci, :, cols] = jnp.where(sel, 0.0, NEG_INF)


def _sidx(page_table, q_rows, wsel, idx_new, cache, topk):
    bd, n_pages = page_table.shape
    _, page, dim = cache.shape
    t_new = idx_new.shape[1]
    cp = _pages_per_chunk(n_pages)
    nc = n_pages // cp
    ck = cp * page
    grid_spec = pltpu.PrefetchScalarGridSpec(
        num_scalar_prefetch=1,
        grid=(bd,),
        in_specs=[pl.BlockSpec((None,) + q_rows.shape[1:], lambda b, pt: (b, 0, 0)),
                  pl.BlockSpec((None,) + wsel.shape[1:], lambda b, pt: (b, 0, 0)),
                  pl.BlockSpec((None, t_new, dim), lambda b, pt: (b, 0, 0)),
                  pl.BlockSpec(memory_space=pl.ANY)],
        out_specs=pl.BlockSpec((None, nc + 1, SCORE_ROWS, ck), lambda b, pt: (b, 0, 0, 0)),
        scratch_shapes=[pltpu.VMEM((2, ck, dim), F32), pltpu.SemaphoreType.DMA((2,)),
                        pltpu.VMEM((nc + 1, SCORE_ROWS, ck), F32), pltpu.VMEM((LANES, dim), F32)])
    return pl.pallas_call(
        functools.partial(_sidx_kernel, cp=cp, nc=nc, page=page, topk=topk, t_new=t_new),
        out_shape=jax.ShapeDtypeStruct((bd, nc + 1, SCORE_ROWS, ck), F32),
        grid_spec=grid_spec,
        compiler_params=_cparams(1),
        name="sample_indexer",
    )(page_table, q_rows, wsel, idx_new, cache)


def _sattn_kernel(pt_ref, qm_ref, qd_ref, bias_ref, mnew_ref, dnew_ref, cmla_ref, cdsa_ref,
                  olat_ref, odsa_ref,
                  buf_m, buf_d, sem_m, sem_d, m1, l1, acc1, m2, l2, acc2, tail_m, tail_d,
                  *, cp, nc, page, t_new):
    b = pl.program_id(0)

    def copies_for(s, c, slot):
        return (_page_copies(pt_ref, cmla_ref, buf_m, sem_m, s, c, slot, cp, page, MLA_ROW)
                + _page_copies(pt_ref, cdsa_ref, buf_d, sem_d, s, c, slot, cp, page, 2 * DSA_HEAD_DIM))

    @pl.when(b == 0)
    def _():
        for c in copies_for(0, 0, 0):
            c.start()

    for m_scr, l_scr, acc_scr in ((m1, l1, acc1), (m2, l2, acc2)):
        m_scr[...] = jnp.full(m_scr.shape, NEG_INF, F32)
        l_scr[...] = jnp.zeros(l_scr.shape, F32)
        acc_scr[...] = jnp.zeros(acc_scr.shape, F32)

    q_lat = qm_ref[:, 0:KV_LORA]
    q_rope = qm_ref[:, KV_LORA:MLA_ROW]
    qd = qd_ref[...]

    def attend(rows_m, rows_d, bias, causal):
        c_kv = rows_m[:, 0:KV_LORA].astype(BF16)
        k_rope = rows_m[:, KV_LORA:MLA_ROW].astype(BF16)
        s = (_nt_dot(q_lat, c_kv) + _nt_dot(q_rope, k_rope)) * MLA_SCALE
        if causal:
            tok = lax.broadcasted_iota(jnp.int32, s.shape, 0) // MLA_HEADS
            col = lax.broadcasted_iota(jnp.int32, s.shape, 1)
            s = jnp.where((col <= tok) & (col < t_new), s, NEG_INF)
        _online_softmax_step(s, c_kv, m1, l1, acc1)
        kd = rows_d[:, 0:DSA_HEAD_DIM].astype(BF16)
        vd = rows_d[:, DSA_HEAD_DIM:2 * DSA_HEAD_DIM].astype(BF16)
        lg = _nt_dot(qd, kd) * DSA_SCALE + _expand_rows(bias, t_new, DSA_HEADS)
        _online_softmax_step(lg, vd, m2, l2, acc2)

    def chunk(ci, carry):
        slot = _ring_step(b, ci, nc, copies_for)
        attend(buf_m[slot], buf_d[slot], bias_ref[ci], False)
        return carry

    lax.fori_loop(0, nc, chunk, 0)

    tail_m[...] = jnp.zeros(tail_m.shape, F32)
    tail_m[0:t_new, :] = mnew_ref[...]
    tail_d[...] = jnp.zeros(tail_d.shape, F32)
    tail_d[0:t_new, :] = dnew_ref[...]
    attend(tail_m[...], tail_d[...], bias_ref[nc, :, 0:LANES], True)

    olat_ref[...] = (acc1[...] / l1[...]).astype(BF16)
    odsa_ref[...] = (acc2[...] / l2[...]).astype(BF16)


def _sattn(page_table, q_mla, q_dsa, bias, mla_new, dsa_new, cache_mla, cache_dsa):
    bd, n_pages = page_table.shape
    page = cache_mla.shape[1]
    t_new = mla_new.shape[1]
    rows = q_mla.shape[1]
    cp = _pages_per_chunk(n_pages)
    nc = n_pages // cp
    ck = cp * page
    blk = lambda a: pl.BlockSpec((None,) + a.shape[1:], lambda b, pt: (b,) + (0,) * (a.ndim - 1))
    grid_spec = pltpu.PrefetchScalarGridSpec(
        num_scalar_prefetch=1,
        grid=(bd,),
        in_specs=[blk(q_mla), blk(q_dsa), blk(bias), blk(mla_new), blk(dsa_new),
                  pl.BlockSpec(memory_space=pl.ANY), pl.BlockSpec(memory_space=pl.ANY)],
        out_specs=(pl.BlockSpec((None, rows, KV_LORA), lambda b, pt: (b, 0, 0)),
                   pl.BlockSpec((None, rows, DSA_HEAD_DIM), lambda b, pt: (b, 0, 0))),
        scratch_shapes=[pltpu.VMEM((2, ck, MLA_ROW), F32), pltpu.VMEM((2, ck, 2 * DSA_HEAD_DIM), F32),
                        pltpu.SemaphoreType.DMA((2,)), pltpu.SemaphoreType.DMA((2,)),
                        pltpu.VMEM((rows, 1), F32), pltpu.VMEM((rows, 1), F32), pltpu.VMEM((rows, KV_LORA), F32),
                        pltpu.VMEM((rows, 1), F32), pltpu.VMEM((rows, 1), F32),
                        pltpu.VMEM((rows, DSA_HEAD_DIM), F32),
                        pltpu.VMEM((LANES, MLA_ROW), F32), pltpu.VMEM((LANES, 2 * DSA_HEAD_DIM), F32)])
    return pl.pallas_call(
        functools.partial(_sattn_kernel, cp=cp, nc=nc, page=page, t_new=t_new),
        out_shape=(jax.ShapeDtypeStruct((bd, rows, KV_LORA), BF16),
                   jax.ShapeDtypeStruct((bd, rows, DSA_HEAD_DIM), BF16)),
        grid_spec=grid_spec,
        compiler_params=_cparams(1),
        name="sample_attention",
    )(page_table, q_mla, q_dsa, bias, mla_new, dsa_new, cache_mla, cache_dsa)


ROW_TILE = 256
PROJ_TILE = 512
ATTN_TILE = 256
MLA_HEAD_GROUPS = 2


def _tile(n, target):
    t = min(n, target)
    assert n % t == 0
    return t


def _prompt_layer(x, mod, weights):
    batch, seq, d = x.shape
    shift, scale, gate = mod
    g_pre, g_post, w_in_p, g_q, w_uq_p, g_kv, w_ukv_bf, w_out_bf = weights
    m = batch * seq
    x2 = x.reshape(m, d)
    tr, tm, tq = _tile(seq, ROW_TILE), _tile(seq, PROJ_TILE), _tile(seq, ATTN_TILE)
    pos = jnp.arange(seq)
    c128, s128 = _rope_tables(pos, DSA_HEAD_DIM // 2)
    c64, s64 = _rope_tables(pos, MLA_ROPE // 2)
    per_batch = lambda a: a[:, None, :]
    group = lambda i: i // (seq // tr)

    h = _prenorm(x2, g_pre, per_batch(scale), per_batch(shift), tr, group)
    cq, sz, qd, qi, mla_row, dsa_row, idx_row, misc = _inproj(
        h, w_in_p, (c128, s128, c64, s64), g_q, g_kv, tm, seq // tm, True)
    q_hm = _uq(cq, w_uq_p, c64, s64, tm, seq // tm, True)
    k_hm, v_hm = _kvup(mla_row, misc, w_ukv_bf, tm)
    o_mla = _mla_prompt(q_hm, k_hm, v_hm, batch, seq, tq, MLA_HEAD_GROUPS)
    o_dsa = _dsa_prompt(qi, idx_row, misc, qd, dsa_row, batch, seq, tq)
    o_raw = _outproj(o_mla, o_dsa, sz, w_out_bf, tm, True)
    y = _finish(x2, o_raw, per_batch(gate), g_post, tr, group)
    unflat = lambda a: a.reshape(batch, seq, a.shape[-1])
    return unflat(y), unflat(mla_row), unflat(dsa_row), unflat(idx_row)


def _sample_layer(x, mod, weights, cache_mla, cache_dsa, cache_idx, page_table):
    bd, t_new, d = x.shape
    shift, scale, gate = mod
    g_pre, g_post, w_in_p, g_q, w_uq_p, g_kv, w_ukv_bf, w_out_bf = weights
    m = bd * t_new
    assert t_new <= SCORE_ROWS and t_new * IDX_HEADS == LANES
    x2 = x.reshape(m, d)
    tr, tm = _tile(m, ROW_TILE), _tile(m, PROJ_TILE)
    past = page_table.shape[1] * cache_mla.shape[1]
    pos = past + jnp.arange(t_new)
    per_row = lambda tab: jnp.tile(tab, (bd, 1))
    c128, s128 = map(per_row, _rope_tables(pos, DSA_HEAD_DIM // 2))
    c64, s64 = map(per_row, _rope_tables(pos, MLA_ROPE // 2))
    per_token = lambda a: jnp.repeat(a, t_new, axis=0).reshape(m // tr, tr, d)
    group = lambda i: i

    h = _prenorm(x2, g_pre, per_token(scale), per_token(shift), tr, group)
    cq, sz, qd, qi, mla_new, dsa_new, idx_new, misc = _inproj(
        h, w_in_p, (c128, s128, c64, s64), g_q, g_kv, tm, m // tm, False)
    q_nope, q_rope = _uq(cq, w_uq_p, c64, s64, tm, m // tm, False)
    q_mla = _absorb(q_nope, q_rope, w_ukv_bf).reshape(bd, t_new * MLA_HEADS, Q_MLA_W)

    w_idx = misc[:, W_IDX_LANE0:W_IDX_LANE0 + IDX_HEADS].reshape(bd, 1, t_new, IDX_HEADS)
    wsel = (jnp.eye(t_new, dtype=F32)[None, :, :, None] * w_idx).reshape(bd, t_new, t_new * IDX_HEADS)
    wsel = jnp.pad(wsel, ((0, 0), (0, 2 * SCORE_ROWS - t_new), (0, 0))).astype(BF16)

    topk = min(TOPK_MAX, (past + t_new) // 4)
    per_seq = lambda a, rows: a.reshape(bd, rows, -1)
    bias = _sidx(page_table, per_seq(qi, t_new * IDX_HEADS), wsel, per_seq(idx_new, t_new), cache_idx, topk)
    o_lat, o_dsa = _sattn(page_table, q_mla, per_seq(qd, t_new * DSA_HEADS), bias,
                          per_seq(mla_new, t_new), per_seq(dsa_new, t_new), cache_mla, cache_dsa)
    o_mla = _unabsorb(o_lat.reshape(m, MLA_HEADS * KV_LORA), w_ukv_bf)
    o_raw = _outproj(o_mla, o_dsa.reshape(m, DSA_WIDTH), sz, w_out_bf, tm, False)
    y = _finish(x2, o_raw, per_token(gate), g_post, tr, group)
    unflat = lambda a: a.reshape(bd, t_new, a.shape[-1])
    return unflat(y), unflat(mla_new), unflat(dsa_new), unflat(idx_new)


def kernel(x_prompt, x_sample, cache_mla, cache_dsa_kv, cache_idx_k, page_table, c_prompt, c_sample,
           g_pre, g_post, w_ada, b_ada, w_in, g_q_lora, w_uq, g_kv_lora, w_ukv, w_out):
    depth = w_in.shape[0]
    batch, d = c_prompt.shape
    bd = c_sample.shape[0]
    yp, ys = x_prompt, x_sample
    new_rows = [[] for _ in range(6)]
    for l in range(depth):
        c_all = jnp.concatenate([c_prompt, c_sample], axis=0)
        c_all = jnp.pad(c_all, ((0, -c_all.shape[0] % 8), (0, 0)))
        ada = _ada(c_all, w_ada[l], b_ada[l][None])
        mods = [ada[:, k * d:(k + 1) * d] for k in range(3)]
        mod_p = [a[:batch] for a in mods]
        mod_s = [a[batch:batch + bd] for a in mods]
        row = lambda a: a[l][None]
        weights = (row(g_pre), row(g_post), _permute_w_in(w_in[l]), row(g_q_lora), _permute_w_uq(w_uq[l]),
                   row(g_kv_lora), w_ukv[l].astype(BF16), w_out[l].astype(BF16))
        yp, mla_p, dsa_p, idx_p = _prompt_layer(yp, mod_p, weights)
        ys, mla_s, dsa_s, idx_s = _sample_layer(ys, mod_s, weights, cache_mla[l], cache_dsa_kv[l],
                                                cache_idx_k[l], page_table)
        for acc, r in zip(new_rows, (mla_p, dsa_p, idx_p, mla_s, dsa_s, idx_s)):
            acc.append(r)
    return (yp, ys) + tuple(jnp.stack(r) for r in new_rows)
```

```python
import functools

import jax
import jax.numpy as jnp
from jax import lax
from jax.experimental import pallas as pl
from jax.experimental.pallas import tpu as pltpu

MLA_HEADS = 16
MLA_NOPE = 128
MLA_ROPE = 64
MLA_V = 128
Q_LORA = 1024
KV_LORA = 512
DSA_HEADS = 16
DSA_HEAD_DIM = 128
IDX_HEADS = 32
IDX_DIM = 128
TOPK_MAX = 256
ROPE_THETA = 10000.0
EPS = 1e-6

MLA_WIDTH = MLA_HEADS * MLA_V
DSA_WIDTH = DSA_HEADS * DSA_HEAD_DIM
MIX_WIDTH = MLA_WIDTH + DSA_WIDTH
MLA_ROW = KV_LORA + MLA_ROPE
MLA_QK = MLA_NOPE + MLA_ROPE
MLA_SCALE = MLA_QK ** -0.5
DSA_SCALE = DSA_HEAD_DIM ** -0.5
IDX_SCALE = IDX_DIM ** -0.5
IDX_W_SCALE = IDX_HEADS ** -0.5

LANES = 128
IN_TILE = 1024
N_IN_TILES = 12
W_IDX_LANE0 = MLA_ROPE
VMEM_LIMIT = 56 * 1024 * 1024

F32 = jnp.float32
BF16 = jnp.bfloat16
NEG_INF = float("-inf")


def _cparams(n_axes):
    return pltpu.CompilerParams(dimension_semantics=("arbitrary",) * n_axes, vmem_limit_bytes=VMEM_LIMIT)


def _silu(x):
    return x / (1.0 + jnp.exp(-x))


def _rms(x, g):
    return x * lax.rsqrt(jnp.mean(x * x, axis=-1, keepdims=True) + EPS) * g


def _lane_iota(shape):
    return lax.broadcasted_iota(jnp.int32, shape, len(shape) - 1)


def _rope_full(a, cos, sin):
    return a * cos + pltpu.roll(a, LANES // 2, 1) * sin


def _rope_pair(a, cos, sin):
    lane = _lane_iota(a.shape)
    fwd = pltpu.roll(a, LANES - MLA_ROPE // 2, 1)
    bwd = pltpu.roll(a, MLA_ROPE // 2, 1)
    partner = jnp.where((lane % MLA_ROPE) < MLA_ROPE // 2, fwd, bwd)
    return a * cos + partner * sin


def _rope_tables(pos, half):
    inv = ROPE_THETA ** (-jnp.arange(half, dtype=F32) / half)
    ang = pos.astype(F32)[:, None] * inv[None, :]
    cos, sin = jnp.cos(ang), jnp.sin(ang)
    reps = LANES // (2 * half)
    return (jnp.tile(jnp.concatenate([cos, cos], -1), (1, reps)),
            jnp.tile(jnp.concatenate([-sin, sin], -1), (1, reps)))


def _ada_kernel(c_ref, w_ref, b_ref, o_ref):
    s = _silu(c_ref[...]).astype(BF16)
    o_ref[...] = jnp.dot(s, w_ref[...].astype(BF16), preferred_element_type=F32) + b_ref[...]


def _ada(c_all, w_ada, b_ada):
    rows, d = c_all.shape
    n = w_ada.shape[1]
    tn = 512
    return pl.pallas_call(
        _ada_kernel,
        out_shape=jax.ShapeDtypeStruct((rows, n), F32),
        grid=(n // tn,),
        in_specs=[pl.BlockSpec((rows, d), lambda j: (0, 0)),
                  pl.BlockSpec((d, tn), lambda j: (0, j)),
                  pl.BlockSpec((1, tn), lambda j: (0, j))],
        out_specs=pl.BlockSpec((rows, tn), lambda j: (0, j)),
        compiler_params=_cparams(1),
        name="ada",
    )(c_all, w_ada, b_ada)


def _prenorm_kernel(x_ref, g_ref, sc_ref, sh_ref, h_ref):
    h_ref[...] = (_rms(x_ref[...], g_ref[...]) * (1.0 + sc_ref[...]) + sh_ref[...]).astype(BF16)


def _prenorm(x2, g_pre, scale3, shift3, tr, group_of_tile):
    m, d = x2.shape
    r = scale3.shape[1]
    mod_spec = pl.BlockSpec((None, r, d), lambda i: (group_of_tile(i), 0, 0))
    return pl.pallas_call(
        _prenorm_kernel,
        out_shape=jax.ShapeDtypeStruct((m, d), BF16),
        grid=(m // tr,),
        in_specs=[pl.BlockSpec((tr, d), lambda i: (i, 0)),
                  pl.BlockSpec((1, d), lambda i: (0, 0)),
                  mod_spec, mod_spec],
        out_specs=pl.BlockSpec((tr, d), lambda i: (i, 0)),
        compiler_params=_cparams(1),
        name="prenorm",
    )(x2, g_pre, scale3, shift3)


def _permute_w_in(w_in):
    o_cq = 0
    o_ckv = o_cq + Q_LORA
    o_kr = o_ckv + KV_LORA
    o_zm = o_kr + MLA_ROPE
    o_qd = o_zm + MLA_WIDTH
    o_kd = o_qd + DSA_WIDTH
    o_vd = o_kd + DSA_HEAD_DIM
    o_zd = o_vd + DSA_HEAD_DIM
    o_qi = o_zd + DSA_WIDTH
    o_ki = o_qi + IDX_HEADS * IDX_DIM
    o_wi = o_ki + IDX_DIM
    end = o_wi + IDX_HEADS
    assert end == w_in.shape[1]
    w_t = jnp.swapaxes(w_in, 0, 1)
    seg = lambda a, b: w_t[a:b]
    pad = jnp.zeros((LANES - MLA_ROPE - IDX_HEADS, w_in.shape[0]), w_in.dtype)
    rows = [seg(o_cq, o_ckv), seg(o_zm, o_qd), seg(o_qd, o_kd), seg(o_zd, o_qi), seg(o_qi, o_ki),
            seg(o_ckv, o_kr), seg(o_kd, o_vd), seg(o_vd, o_zd), seg(o_ki, o_wi), seg(o_kr, o_zm),
            seg(o_wi, end), pad]
    w = jnp.concatenate(rows, axis=0).astype(BF16)
    assert w.shape[0] == N_IN_TILES * IN_TILE
    return w


def _inproj_kernel(h_ref, w_ref, c128_ref, s128_ref, c64_ref, s64_ref, gq_ref, gkv_ref,
                   cq_ref, sz_ref, qd_ref, qi_ref, mla_ref, dsa_ref, idx_ref, misc_ref, *, head_major):
    j = pl.program_id(1)
    acc = lax.dot_general(h_ref[...], w_ref[...], (((1,), (1,)), ((), ())), preferred_element_type=F32)
    heads_per_tile = IN_TILE // LANES

    def store_heads(ref):
        cos, sin = c128_ref[...], s128_ref[...]
        for hh in range(heads_per_tile):
            r = _rope_full(acc[:, hh * LANES:(hh + 1) * LANES], cos, sin).astype(BF16)
            if head_major:
                ref[hh] = r
            else:
                ref[:, hh * LANES:(hh + 1) * LANES] = r

    @pl.when(j == 0)
    def _():
        cq_ref[...] = _rms(acc, gq_ref[...]).astype(BF16)

    @pl.when((j == 1) | (j == 2) | (j == 5) | (j == 6))
    def _():
        sz_ref[...] = _silu(acc).astype(BF16)

    @pl.when((j == 3) | (j == 4))
    def _():
        store_heads(qd_ref)

    @pl.when((j >= 7) & (j <= 10))
    def _():
        store_heads(qi_ref)

    @pl.when(j == N_IN_TILES - 1)
    def _():
        cos, sin = c128_ref[...], s128_ref[...]
        o_kd = KV_LORA
        o_vd = o_kd + DSA_HEAD_DIM
        o_ki = o_vd + DSA_HEAD_DIM
        o_misc = o_ki + IDX_DIM
        g = acc[:, o_misc:o_misc + LANES]
        lane = _lane_iota(g.shape)
        misc = jnp.where(lane < MLA_ROPE, _rope_pair(g, c64_ref[...], s64_ref[...]), g * IDX_W_SCALE)
        mla_ref[:, 0:KV_LORA] = _rms(acc[:, 0:KV_LORA], gkv_ref[...])
        mla_ref[:, KV_LORA:MLA_ROW] = misc[:, 0:MLA_ROPE]
        dsa_ref[:, 0:DSA_HEAD_DIM] = _rope_full(acc[:, o_kd:o_vd], cos, sin)
        dsa_ref[:, DSA_HEAD_DIM:2 * DSA_HEAD_DIM] = acc[:, o_vd:o_ki]
        idx_ref[...] = _rope_full(acc[:, o_ki:o_misc], cos, sin)
        misc_ref[...] = misc


def _inproj(h, w_perm, tabs, g_q, g_kv, tm, tiles_per_seq, head_major):
    m, d = h.shape
    c128, s128, c64, s64 = tabs
    hpt = IN_TILE // LANES
    ge = lambda j, k: (j >= k).astype(jnp.int32)
    tab_spec = pl.BlockSpec((tm, LANES), lambda i, j: (i % tiles_per_seq, 0))
    row_spec = lambda w: pl.BlockSpec((tm, w), lambda i, j: (i, 0))
    if head_major:
        qd_shape, qi_shape = (DSA_HEADS, m, LANES), (IDX_HEADS, m, LANES)
        qd_spec = pl.BlockSpec((hpt, tm, LANES), lambda i, j: (ge(j, 4), i, 0))
        qi_spec = pl.BlockSpec((hpt, tm, LANES), lambda i, j: (jnp.clip(j - 7, 0, 3), i, 0))
    else:
        qd_shape, qi_shape = (m, DSA_WIDTH), (m, IDX_HEADS * IDX_DIM)
        qd_spec = pl.BlockSpec((tm, IN_TILE), lambda i, j: (i, ge(j, 4)))
        qi_spec = pl.BlockSpec((tm, IN_TILE), lambda i, j: (i, jnp.clip(j - 7, 0, 3)))
    out_shape = (jax.ShapeDtypeStruct((m, Q_LORA), BF16),
                 jax.ShapeDtypeStruct((m, MIX_WIDTH), BF16),
                 jax.ShapeDtypeStruct(qd_shape, BF16),
                 jax.ShapeDtypeStruct(qi_shape, BF16),
                 jax.ShapeDtypeStruct((m, MLA_ROW), F32),
                 jax.ShapeDtypeStruct((m, 2 * DSA_HEAD_DIM), F32),
                 jax.ShapeDtypeStruct((m, IDX_DIM), F32),
                 jax.ShapeDtypeStruct((m, LANES), F32))
    out_specs = (row_spec(Q_LORA),
                 pl.BlockSpec((tm, IN_TILE), lambda i, j: (i, ge(j, 2) + ge(j, 5) + ge(j, 6))),
                 qd_spec, qi_spec,
                 row_spec(MLA_ROW), row_spec(2 * DSA_HEAD_DIM), row_spec(IDX_DIM), row_spec(LANES))
    return pl.pallas_call(
        functools.partial(_inproj_kernel, head_major=head_major),
        out_shape=out_shape,
        grid=(m // tm, N_IN_TILES),
        in_specs=[pl.BlockSpec((tm, d), lambda i, j: (i, 0)),
                  pl.BlockSpec((IN_TILE, d), lambda i, j: (j, 0)),
                  tab_spec, tab_spec, tab_spec, tab_spec,
                  pl.BlockSpec((1, Q_LORA), lambda i, j: (0, 0)),
                  pl.BlockSpec((1, KV_LORA), lambda i, j: (0, 0))],
        out_specs=out_specs,
        compiler_params=_cparams(2),
        name="inproj",
    )(h, w_perm, c128, s128, c64, s64, g_q, g_kv)


def _permute_w_uq(w_uq):
    w3 = w_uq.reshape(Q_LORA, MLA_HEADS, MLA_QK)
    nope = w3[:, :, :MLA_NOPE].reshape(Q_LORA, MLA_HEADS * MLA_NOPE)
    rope = w3[:, :, MLA_NOPE:].reshape(Q_LORA, MLA_HEADS * MLA_ROPE)
    return jnp.concatenate([nope, rope], axis=1).astype(BF16)


def _uq_kernel(cq_ref, w_ref, c64_ref, s64_ref, *out_refs, head_major):
    acc = jnp.dot(cq_ref[...], w_ref[...], preferred_element_type=F32)
    cos, sin = c64_ref[...], s64_ref[...]
    rope0 = MLA_HEADS * MLA_NOPE
    if head_major:
        (q_ref,) = out_refs
        for h in range(MLA_HEADS):
            q_ref[h, :, 0:MLA_NOPE] = acc[:, h * MLA_NOPE:(h + 1) * MLA_NOPE].astype(BF16)
        for p in range(MLA_HEADS // 2):
            r = _rope_pair(acc[:, rope0 + p * LANES:rope0 + (p + 1) * LANES], cos, sin)
            lane = _lane_iota(r.shape)
            q_ref[2 * p, :, MLA_NOPE:2 * MLA_NOPE] = jnp.where(lane < MLA_ROPE, r, 0.0).astype(BF16)
            q_ref[2 * p + 1, :, MLA_NOPE:2 * MLA_NOPE] = jnp.where(
                lane < MLA_ROPE, pltpu.roll(r, MLA_ROPE, 1), 0.0).astype(BF16)
    else:
        qn_ref, qr_ref = out_refs
        qn_ref[...] = acc[:, 0:rope0].astype(BF16)
        for p in range(MLA_HEADS // 2):
            r = _rope_pair(acc[:, rope0 + p * LANES:rope0 + (p + 1) * LANES], cos, sin)
            qr_ref[:, p * LANES:(p + 1) * LANES] = r.astype(BF16)


def _uq(cq, w_uq_perm, c64, s64, tm, tiles_per_seq, head_major):
    m = cq.shape[0]
    n = w_uq_perm.shape[1]
    if head_major:
        out_shape = jax.ShapeDtypeStruct((MLA_HEADS, m, 2 * MLA_NOPE), BF16)
        out_specs = pl.BlockSpec((MLA_HEADS, tm, 2 * MLA_NOPE), lambda i: (0, i, 0))
    else:
        out_shape = (jax.ShapeDtypeStruct((m, MLA_HEADS * MLA_NOPE), BF16),
                     jax.ShapeDtypeStruct((m, MLA_HEADS * MLA_ROPE), BF16))
        out_specs = (pl.BlockSpec((tm, MLA_HEADS * MLA_NOPE), lambda i: (i, 0)),
                     pl.BlockSpec((tm, MLA_HEADS * MLA_ROPE), lambda i: (i, 0)))
    tab_spec = pl.BlockSpec((tm, LANES), lambda i: (i % tiles_per_seq, 0))
    return pl.pallas_call(
        functools.partial(_uq_kernel, head_major=head_major),
        out_shape=out_shape,
        grid=(m // tm,),
        in_specs=[pl.BlockSpec((tm, Q_LORA), lambda i: (i, 0)),
                  pl.BlockSpec((Q_LORA, n), lambda i: (0, 0)),
                  tab_spec, tab_spec],
        out_specs=out_specs,
        compiler_params=_cparams(1),
        name="uq",
    )(cq, w_uq_perm, c64, s64)


def _kvup_kernel(mla_ref, misc_ref, w_ref, k_ref, vt_ref):
    c = mla_ref[:, 0:KV_LORA].astype(BF16)
    kv = jnp.dot(c, w_ref[...], preferred_element_type=F32)
    misc = misc_ref[...]
    krz = jnp.where(_lane_iota(misc.shape) < MLA_ROPE, misc, 0.0).astype(BF16)
    per_head = MLA_NOPE + MLA_V
    for h in range(MLA_HEADS):
        k_ref[h, :, 0:MLA_NOPE] = kv[:, h * per_head:h * per_head + MLA_NOPE].astype(BF16)
        k_ref[h, :, MLA_NOPE:2 * MLA_NOPE] = krz
        vt_ref[h, 0] = kv[:, h * per_head + MLA_NOPE:(h + 1) * per_head].T.astype(BF16)


def _kvup(mla_row, misc, w_ukv_bf, tm):
    m = mla_row.shape[0]
    n = w_ukv_bf.shape[1]
    return pl.pallas_call(
        _kvup_kernel,
        out_shape=(jax.ShapeDtypeStruct((MLA_HEADS, m, 2 * MLA_NOPE), BF16),
                   jax.ShapeDtypeStruct((MLA_HEADS, m // tm, MLA_V, tm), BF16)),
        grid=(m // tm,),
        in_specs=[pl.BlockSpec((tm, MLA_ROW), lambda i: (i, 0)),
                  pl.BlockSpec((tm, LANES), lambda i: (i, 0)),
                  pl.BlockSpec((KV_LORA, n), lambda i: (0, 0))],
        out_specs=(pl.BlockSpec((MLA_HEADS, tm, 2 * MLA_NOPE), lambda i: (0, i, 0)),
                   pl.BlockSpec((MLA_HEADS, 1, MLA_V, tm), lambda i: (0, i, 0, 0))),
        compiler_params=_cparams(1),
        name="kvup",
    )(mla_row, misc, w_ukv_bf)


def _nt_dot(a, b):
    return lax.dot_general(a, b, (((1,), (1,)), ((), ())), preferred_element_type=F32)


def _online_softmax_step(s, v, m_scr, l_scr, acc_scr, v_transposed=False):
    m_prev = m_scr[...]
    m_new = jnp.maximum(m_prev, jnp.max(s, axis=-1, keepdims=True))
    m_safe = jnp.where(m_new == NEG_INF, 0.0, m_new)
    alpha = jnp.exp(m_prev - m_safe)
    p = jnp.exp(s - m_safe)
    pv = _nt_dot(p.astype(BF16), v) if v_transposed else jnp.dot(p.astype(BF16), v, preferred_element_type=F32)
    l_scr[...] = alpha * l_scr[...] + jnp.sum(p, axis=-1, keepdims=True)
    acc_scr[...] = alpha * acc_scr[...] + pv
    m_scr[...] = m_new


MLA_HEADS_PER_ITER = 2


def _mla_prompt_kernel(q_ref, k_ref, vt_ref, o_ref, m_scr, l_scr, acc_scr, *, tq):
    qi = pl.program_id(2)
    heads = q_ref.shape[0]
    lanes = range(MLA_HEADS_PER_ITER)

    def head_group(hp, carry):
        for u in lanes:
            m_scr[u] = jnp.full(m_scr.shape[1:], NEG_INF, F32)
            l_scr[u] = jnp.zeros(l_scr.shape[1:], F32)
            acc_scr[u] = jnp.zeros(acc_scr.shape[1:], F32)

        def chunk(c, diagonal):
            start = pl.multiple_of(c * tq, tq)
            for u in lanes:
                h = hp * MLA_HEADS_PER_ITER + u
                st = _nt_dot(k_ref[h, pl.ds(start, tq), :], q_ref[h]) * MLA_SCALE
                if diagonal:
                    key = lax.broadcasted_iota(jnp.int32, st.shape, 0)
                    qry = lax.broadcasted_iota(jnp.int32, st.shape, 1)
                    st = jnp.where(key <= qry, st, NEG_INF)
                m_prev = m_scr[u]
                m_new = jnp.maximum(m_prev, jnp.max(st, axis=0, keepdims=True))
                alpha = jnp.exp(m_prev - m_new)
                p = jnp.exp(st - m_new)
                l_scr[u] = alpha * l_scr[u] + jnp.sum(p, axis=0, keepdims=True)
                acc_scr[u] = alpha * acc_scr[u] + jnp.dot(vt_ref[h, c], p.astype(BF16),
                                                          preferred_element_type=F32)
                m_scr[u] = m_new

        def below(c, carry2):
            chunk(c, False)
            return carry2

        chunk(qi, True)
        lax.fori_loop(0, qi, below, 0)
        for u in lanes:
            o_ref[hp * MLA_HEADS_PER_ITER + u] = (acc_scr[u] / l_scr[u]).T.astype(BF16)
        return carry

    lax.fori_loop(0, heads // MLA_HEADS_PER_ITER, head_group, 0)


def _mla_prompt(q_hm, k_hm, vt_hm, batch, seq, tq, head_groups):
    heads, m, dq = q_hm.shape
    assert vt_hm.shape == (heads, m // tq, MLA_V, tq)
    hg = heads // head_groups
    nq = seq // tq
    return pl.pallas_call(
        functools.partial(_mla_prompt_kernel, tq=tq),
        out_shape=jax.ShapeDtypeStruct((heads, m, MLA_V), BF16),
        grid=(batch, head_groups, nq),
        in_specs=[pl.BlockSpec((hg, tq, dq), lambda b, g, i: (g, b * nq + i, 0)),
                  pl.BlockSpec((hg, seq, dq), lambda b, g, i: (g, b, 0)),
                  pl.BlockSpec((hg, nq, MLA_V, tq), lambda b, g, i: (g, b, 0, 0))],
        out_specs=pl.BlockSpec((hg, tq, MLA_V), lambda b, g, i: (g, b * nq + i, 0)),
        scratch_shapes=[pltpu.VMEM((MLA_HEADS_PER_ITER, 1, tq), F32), pltpu.VMEM((MLA_HEADS_PER_ITER, 1, tq), F32),
                        pltpu.VMEM((MLA_HEADS_PER_ITER, MLA_V, tq), F32)],
        compiler_params=_cparams(3),
        name="mla_prompt",
    )(q_hm, k_hm, vt_hm)


INT_MIN = -2 ** 31
KEY_NEG_INF = INT_MIN + 0x7FFFFF


def _sort_key(x):
    x = jnp.where(x == 0.0, 0.0, x)
    b = lax.bitcast_convert_type(x, jnp.int32)
    return b ^ ((b >> 31) & 0x7FFFFFFF)


def _select_topk(problems, k, count, stat_shape, pos_bits, real=None):
    n = len(problems)
    keys = [p[0] for p in problems]

    def value_bit(it, prefixes):
        bit = jnp.left_shift(jnp.int32(1), 31 - it)
        out = []
        for key, prefix in zip(keys, prefixes):
            cand_u = prefix | bit
            out.append(jnp.where(count(key >= (cand_u ^ INT_MIN)) >= k, cand_u, prefix))
        return tuple(out)

    zeros = tuple(jnp.zeros(stat_shape, jnp.int32) for _ in range(n))
    thrs = [t ^ INT_MIN for t in lax.fori_loop(0, 32, value_bit, zeros)]
    flags = []
    for key, thr in zip(keys, thrs):
        tied = (count(key >= thr) > k) & (thr > KEY_NEG_INF)
        flags.append(jnp.max(jnp.where(tied if real is None else tied & real, 1.0, 0.0)))
    has_ties = functools.reduce(jnp.maximum, flags)

    @pl.when(has_ties == 0.0)
    def _():
        for (key, _, store), thr in zip(problems, thrs):
            store(jnp.where((key >= thr) & (key > KEY_NEG_INF), 0.0, NEG_INF))

    @pl.when(has_ties > 0.0)
    def _():
        needs = [k - count(key > thr) for key, thr in zip(keys, thrs)]

        def pos_bit(it, bounds):
            bit = jnp.left_shift(jnp.int32(1), pos_bits - 1 - it)
            out = []
            for (key, pos, _), thr, need, bound in zip(problems, thrs, needs, bounds):
                cand = bound | bit
                out.append(jnp.where(count((key == thr) & (pos < cand)) <= need, cand, bound))
            return tuple(out)

        bounds = lax.fori_loop(0, pos_bits, pos_bit, zeros)
        for (key, pos, store), thr, bound in zip(problems, thrs, bounds):
            sel = ((key > thr) | ((key == thr) & (pos < bound))) & (key > KEY_NEG_INF)
            store(jnp.where(sel, 0.0, NEG_INF))


DSA_HEADS_PER_ITER = 2


def _dsa_prompt_kernel(qi_ref, kidx_ref, misc_ref, qd_ref, kv_ref, o_ref,
                       score_scr, bias_scr, wt_scr, kidx_scr, kd_scr, vt_scr, *, tq, topk, class_keys):
    i = pl.program_id(1)
    seq = kidx_ref.shape[0]
    wt_scr[...] = misc_ref[...].T * IDX_SCALE

    def attend(nk):
        keys = slice(0, nk)
        kidx_scr[keys] = kidx_ref[keys].astype(BF16)
        kd_scr[keys] = kv_ref[keys, 0:DSA_HEAD_DIM].astype(BF16)
        vt_scr[:, keys] = kv_ref[keys, DSA_HEAD_DIM:2 * DSA_HEAD_DIM].T.astype(BF16)
        score_scr[keys] = jnp.zeros((nk, tq), F32)

        def idx_heads(hp, carry):
            part = None
            for u in range(DSA_HEADS_PER_ITER):
                h = hp * DSA_HEADS_PER_ITER + u
                d = _nt_dot(kidx_scr[keys], qi_ref[h])
                term = jnp.maximum(d, 0.0) * wt_scr[pl.ds(W_IDX_LANE0 + h, 1), :]
                part = term if part is None else part + term
            score_scr[keys] += part
            return carry

        lax.fori_loop(0, IDX_HEADS // DSA_HEADS_PER_ITER, idx_heads, 0)

        key_pos = lax.broadcasted_iota(jnp.int32, (nk, tq), 0)
        q_pos = i * tq + lax.broadcasted_iota(jnp.int32, (nk, tq), 1)
        key = _sort_key(jnp.where(key_pos <= q_pos, score_scr[keys], NEG_INF))
        count = lambda mask: jnp.sum(jnp.where(mask, 1.0, 0.0), axis=0, keepdims=True)

        def store(bias):
            bias_scr[keys] = bias

        _select_topk([(key, key_pos, store)], topk, count, (1, tq), seq.bit_length())

        def attn_heads(hp, carry):
            for u in range(DSA_HEADS_PER_ITER):
                h = hp * DSA_HEADS_PER_ITER + u
                lg = _nt_dot(kd_scr[keys], qd_ref[h]) * DSA_SCALE + bias_scr[keys]
                e = jnp.exp(lg - jnp.max(lg, axis=0, keepdims=True))
                l = jnp.sum(e, axis=0, keepdims=True)
                ot = jnp.dot(vt_scr[:, keys], e.astype(BF16), preferred_element_type=F32)
                o_ref[h] = (ot / l).T.astype(BF16)
            return carry

        lax.fori_loop(0, DSA_HEADS // DSA_HEADS_PER_ITER, attn_heads, 0)

    cls = (i * tq) // class_keys
    for c in range(seq // class_keys):
        pl.when(cls == c)(functools.partial(attend, (c + 1) * class_keys))


def _dsa_class_keys(seq, tq, topk):
    for n_classes in (4, 2, 1):
        ck = seq // n_classes
        if seq % n_classes == 0 and ck % tq == 0 and ck >= topk:
            return ck
    raise ValueError("no causal key-extent class fits")


def _dsa_prompt(qi_hm, idx_row, misc, qd_hm, dsa_row, batch, seq, tq):
    m = idx_row.shape[0]
    nq = seq // tq
    topk = min(TOPK_MAX, seq // 4)
    row = lambda b, i: (b * nq + i, 0)
    return pl.pallas_call(
        functools.partial(_dsa_prompt_kernel, tq=tq, topk=topk, class_keys=_dsa_class_keys(seq, tq, topk)),
        out_shape=jax.ShapeDtypeStruct((DSA_HEADS, m, DSA_HEAD_DIM), BF16),
        grid=(batch, nq),
        in_specs=[pl.BlockSpec((IDX_HEADS, tq, IDX_DIM), lambda b, i: (0, b * nq + i, 0)),
                  pl.BlockSpec((seq, IDX_DIM), lambda b, i: (b, 0)),
                  pl.BlockSpec((tq, LANES), row),
                  pl.BlockSpec((DSA_HEADS, tq, DSA_HEAD_DIM), lambda b, i: (0, b * nq + i, 0)),
                  pl.BlockSpec((seq, 2 * DSA_HEAD_DIM), lambda b, i: (b, 0))],
        out_specs=pl.BlockSpec((DSA_HEADS, tq, DSA_HEAD_DIM), lambda b, i: (0, b * nq + i, 0)),
        scratch_shapes=[pltpu.VMEM((seq, tq), F32), pltpu.VMEM((seq, tq), F32),
                        pltpu.VMEM((LANES, tq), F32), pltpu.VMEM((seq, IDX_DIM), BF16),
                        pltpu.VMEM((seq, DSA_HEAD_DIM), BF16), pltpu.VMEM((DSA_HEAD_DIM, seq), BF16)],
        compiler_params=_cparams(2),
        name="dsa_prompt",
    )(qi_hm, idx_row, misc, qd_hm, dsa_row)


def _outproj_kernel(oa_ref, ob_ref, sz_ref, w_ref, o_ref, lhs_scr, *, head_major):
    @pl.when(pl.program_id(1) == 0)
    def _():
        if head_major:
            for h in range(MLA_HEADS):
                cols = slice(h * MLA_V, (h + 1) * MLA_V)
                lhs_scr[:, cols] = oa_ref[h] * sz_ref[:, cols]
            for h in range(DSA_HEADS):
                cols = slice(MLA_WIDTH + h * DSA_HEAD_DIM, MLA_WIDTH + (h + 1) * DSA_HEAD_DIM)
                lhs_scr[:, cols] = ob_ref[h] * sz_ref[:, cols]
        else:
            lhs_scr[:, 0:MLA_WIDTH] = oa_ref[...] * sz_ref[:, 0:MLA_WIDTH]
            lhs_scr[:, MLA_WIDTH:MIX_WIDTH] = ob_ref[...] * sz_ref[:, MLA_WIDTH:MIX_WIDTH]

    o_ref[...] = jnp.dot(lhs_scr[...], w_ref[...], preferred_element_type=F32)


def _outproj(o_mla, o_dsa, sz, w_out_bf, tm, head_major):
    m = sz.shape[0]
    d = w_out_bf.shape[1]
    tn = 1024
    if head_major:
        oa_spec = pl.BlockSpec((MLA_HEADS, tm, MLA_V), lambda i, j: (0, i, 0))
        ob_spec = pl.BlockSpec((DSA_HEADS, tm, DSA_HEAD_DIM), lambda i, j: (0, i, 0))
    else:
        oa_spec = pl.BlockSpec((tm, MLA_WIDTH), lambda i, j: (i, 0))
        ob_spec = pl.BlockSpec((tm, DSA_WIDTH), lambda i, j: (i, 0))
    return pl.pallas_call(
        functools.partial(_outproj_kernel, head_major=head_major),
        out_shape=jax.ShapeDtypeStruct((m, d), F32),
        grid=(m // tm, d // tn),
        in_specs=[oa_spec, ob_spec,
                  pl.BlockSpec((tm, MIX_WIDTH), lambda i, j: (i, 0)),
                  pl.BlockSpec((MIX_WIDTH, tn), lambda i, j: (0, j))],
        out_specs=pl.BlockSpec((tm, tn), lambda i, j: (i, j)),
        scratch_shapes=[pltpu.VMEM((tm, MIX_WIDTH), BF16)],
        compiler_params=_cparams(2),
        name="outproj",
    )(o_mla, o_dsa, sz, w_out_bf)


def _finish_kernel(x_ref, o_ref, gate_ref, g_ref, y_ref):
    y_ref[...] = x_ref[...] + gate_ref[...] * _rms(o_ref[...], g_ref[...])


def _finish(x2, o_raw, gate3, g_post, tr, group_of_tile):
    m, d = x2.shape
    r = gate3.shape[1]
    row_spec = pl.BlockSpec((tr, d), lambda i: (i, 0))
    return pl.pallas_call(
        _finish_kernel,
        out_shape=jax.ShapeDtypeStruct((m, d), F32),
        grid=(m // tr,),
        in_specs=[row_spec, row_spec,
                  pl.BlockSpec((None, r, d), lambda i: (group_of_tile(i), 0, 0)),
                  pl.BlockSpec((1, d), lambda i: (0, 0))],
        out_specs=row_spec,
        compiler_params=_cparams(1),
        name="finish",
    )(x2, o_raw, gate3, g_post)


Q_MLA_W = KV_LORA + LANES


def _absorb_kernel(qn_ref, qr_ref, w_ref, o_ref):
    h = pl.program_id(0)
    o_ref[:, 0:KV_LORA] = _nt_dot(qn_ref[...], w_ref[...]).astype(BF16)
    g = qr_ref[...]
    g = jnp.where(h % 2 == 0, g, pltpu.roll(g.astype(F32), MLA_ROPE, 1).astype(BF16))
    o_ref[:, KV_LORA:Q_MLA_W] = jnp.where(_lane_iota(g.shape) < MLA_ROPE, g, jnp.zeros_like(g))


def _absorb(q_nope, q_rope, w_ukv_bf):
    m = q_nope.shape[0]
    return pl.pallas_call(
        _absorb_kernel,
        out_shape=jax.ShapeDtypeStruct((m, MLA_HEADS * Q_MLA_W), BF16),
        grid=(MLA_HEADS,),
        in_specs=[pl.BlockSpec((m, MLA_NOPE), lambda h: (0, h)),
                  pl.BlockSpec((m, LANES), lambda h: (0, h // 2)),
                  pl.BlockSpec((KV_LORA, MLA_NOPE), lambda h: (0, 2 * h))],
        out_specs=pl.BlockSpec((m, Q_MLA_W), lambda h: (0, h)),
        compiler_params=_cparams(1),
        name="absorb",
    )(q_nope, q_rope, w_ukv_bf)


def _unabsorb_kernel(ol_ref, w_ref, o_ref):
    o_ref[...] = jnp.dot(ol_ref[...], w_ref[...], preferred_element_type=F32).astype(BF16)


def _unabsorb(o_lat, w_ukv_bf):
    m = o_lat.shape[0]
    return pl.pallas_call(
        _unabsorb_kernel,
        out_shape=jax.ShapeDtypeStruct((m, MLA_WIDTH), BF16),
        grid=(MLA_HEADS,),
        in_specs=[pl.BlockSpec((m, KV_LORA), lambda h: (0, h)),
                  pl.BlockSpec((KV_LORA, MLA_V), lambda h: (0, 2 * h + 1))],
        out_specs=pl.BlockSpec((m, MLA_V), lambda h: (0, h)),
        compiler_params=_cparams(1),
        name="unabsorb",
    )(o_lat, w_ukv_bf)


SCORE_ROWS = 8
HALF_ROWS = SCORE_ROWS // 2
MAX_CHUNK_PAGES = 32
ATTN_RING_DEPTH = 2
IDX_RING_PAGES = 128
IDX_RING_DEPTH = 2
SELECT_GROUP = 8


def _pages_per_chunk(n_pages, max_pages=MAX_CHUNK_PAGES):
    cp = max_pages
    while n_pages % cp:
        cp //= 2
    return cp


def _page_copies(pt_ref, cache_ref, buf, sem, seq, chunk, slot, cp, page, width):
    return [pltpu.make_async_copy(cache_ref.at[pt_ref[seq, chunk * cp + p]],
                                  buf.at[slot, pl.ds(p * page, page), pl.ds(0, width)],
                                  sem.at[slot])
            for p in range(cp)]


def _start_all(copies):
    for n, c in enumerate(copies):
        c.start(priority=n % 2)


def _ring_prologue(depth, nc, n_seq, copies_for):
    for j in range(min(depth - 1, nc * n_seq)):
        _start_all(copies_for(j // nc, j % nc, j % depth))


def _ring_step(b, ci, nc, depth, copies_for):
    nb = pl.num_programs(0)
    g = b * nc + ci
    nxt = g + depth - 1
    nxt_b = nxt // nc

    @pl.when(nxt_b < nb)
    def _():
        _start_all(copies_for(nxt_b, nxt - nxt_b * nc, nxt % depth))

    slot = g % depth
    for c in copies_for(b, ci, slot):
        c.wait()
    return slot


def _fold_tiles(nc):
    return (nc + 2) // 2


def _fold_slot(c, ncp):
    half = c // ncp
    return c - half * ncp, slice(half * HALF_ROWS, (half + 1) * HALF_ROWS)


def _sidx_kernel(pt_ref, q_ref, wsel_ref, knew_ref, cache_ref, score_ref,
                 buf, sem, tail_scr, *, cp, ring_chunks, ck, nc, ncp, page, t_new, n_seq):
    b = pl.program_id(0)
    tiles_per_chunk = cp * page // ck
    copies_for = lambda s, c, slot: _page_copies(pt_ref, cache_ref, buf, sem, s, c, slot, cp, page, IDX_DIM)

    @pl.when(b == 0)
    def _():
        _ring_prologue(IDX_RING_DEPTH, ring_chunks, n_seq, copies_for)

    q = q_ref[...]
    wsel = wsel_ref[...]

    def scores(keys_bf):
        r = jnp.maximum(_nt_dot(q, keys_bf), 0.0).astype(BF16)
        return jnp.dot(wsel, r, preferred_element_type=F32)[0:SCORE_ROWS] * IDX_SCALE

    def ring_chunk(cj, carry):
        slot = _ring_step(b, cj, ring_chunks, IDX_RING_DEPTH, copies_for)

        def tile(tj, carry2):
            ci = cj * tiles_per_chunk + tj
            keys = buf[slot, pl.ds(pl.multiple_of(tj * ck, ck), ck), :]
            sc = scores(keys.astype(BF16))

            @pl.when(ci < ncp)
            def _():
                score_ref[ci, 0:HALF_ROWS, :] = sc[0:HALF_ROWS]

            @pl.when(ci >= ncp)
            def _():
                score_ref[ci - ncp, HALF_ROWS:SCORE_ROWS, :] = sc[HALF_ROWS:SCORE_ROWS]

            return carry2

        return lax.fori_loop(0, tiles_per_chunk, tile, carry)

    lax.fori_loop(0, ring_chunks, ring_chunk, 0)

    tail_scr[...] = jnp.zeros(tail_scr.shape, F32)
    tail_scr[0:t_new, :] = knew_ref[...]
    st = scores(tail_scr[...].astype(BF16))
    tok = lax.broadcasted_iota(jnp.int32, st.shape, 0) % HALF_ROWS
    col = lax.broadcasted_iota(jnp.int32, st.shape, 1)
    st = jnp.where((col <= tok) & (col < t_new), st, NEG_INF)
    tile, rows = _fold_slot(nc, ncp)
    score_ref[tile, rows, 0:LANES] = st[rows]
    score_ref[tile, rows, LANES:ck] = jnp.full((HALF_ROWS, ck - LANES), NEG_INF, F32)
    for c in range(nc + 1, 2 * ncp):
        tile, rows = _fold_slot(c, ncp)
        score_ref[tile, rows, :] = jnp.full((HALF_ROWS, ck), NEG_INF, F32)


def _sidx(page_table, q_rows, wsel, idx_new, cache):
    bd, n_pages = page_table.shape
    _, page, dim = cache.shape
    t_new = idx_new.shape[1]
    ck = _pages_per_chunk(n_pages) * page
    nc = n_pages * page // ck
    ncp = _fold_tiles(nc)
    cp = _pages_per_chunk(n_pages, IDX_RING_PAGES)
    grid_spec = pltpu.PrefetchScalarGridSpec(
        num_scalar_prefetch=1,
        grid=(bd,),
        in_specs=[pl.BlockSpec((None,) + q_rows.shape[1:], lambda b, pt: (b, 0, 0)),
                  pl.BlockSpec((None,) + wsel.shape[1:], lambda b, pt: (b, 0, 0)),
                  pl.BlockSpec((None, t_new, dim), lambda b, pt: (b, 0, 0)),
                  pl.BlockSpec(memory_space=pl.ANY)],
        out_specs=pl.BlockSpec((None, ncp, SCORE_ROWS, ck), lambda b, pt: (b, 0, 0, 0)),
        scratch_shapes=[pltpu.VMEM((IDX_RING_DEPTH, cp * page, dim), F32),
                        pltpu.SemaphoreType.DMA((IDX_RING_DEPTH,)),
                        pltpu.VMEM((LANES, dim), F32)])
    return pl.pallas_call(
        functools.partial(_sidx_kernel, cp=cp, ring_chunks=n_pages // cp, ck=ck, nc=nc, ncp=ncp, page=page,
                          t_new=t_new, n_seq=bd),
        out_shape=jax.ShapeDtypeStruct((bd, ncp, SCORE_ROWS, ck), F32),
        grid_spec=grid_spec,
        compiler_params=_cparams(1),
        name="sample_indexer",
    )(page_table, q_rows, wsel, idx_new, cache)


def _sselect_kernel(score_ref, bias_ref, *, ncp, topk, t_new, pos_bits):
    group, _, _, ck = score_ref.shape
    shape = (ncp, SCORE_ROWS, ck)
    tile = lax.broadcasted_iota(jnp.int32, shape, 0)
    row = lax.broadcasted_iota(jnp.int32, shape, 1)
    lane = lax.broadcasted_iota(jnp.int32, shape, 2)
    pos = (tile + jnp.where(row >= HALF_ROWS, ncp, 0)) * ck + lane

    def count(mask):
        x = jnp.sum(jnp.where(mask, 1.0, 0.0), axis=0)
        x = x + pltpu.roll(x, HALF_ROWS, 0)
        return jnp.sum(x, axis=1, keepdims=True)[None]

    def problem(g):
        def store(bias):
            bias_ref[g] = bias
        return _sort_key(score_ref[g]), pos, store

    real = lax.broadcasted_iota(jnp.int32, (1, SCORE_ROWS, 1), 1) % HALF_ROWS < t_new
    _select_topk([problem(g) for g in range(group)], topk, count, (1, SCORE_ROWS, 1), pos_bits, real)


def _sselect(scores, topk, t_new, n_keys):
    bd, ncp, rows, ck = scores.shape
    group = _tile(bd, SELECT_GROUP)
    spec = pl.BlockSpec((group, ncp, rows, ck), lambda i: (i, 0, 0, 0))
    return pl.pallas_call(
        functools.partial(_sselect_kernel, ncp=ncp, topk=topk, t_new=t_new, pos_bits=n_keys.bit_length()),
        out_shape=jax.ShapeDtypeStruct(scores.shape, F32),
        grid=(bd // group,),
        in_specs=[spec],
        out_specs=spec,
        compiler_params=_cparams(1),
        name="sample_select",
    )(scores)


def _sattn_kernel(pt_ref, qm_ref, qd_ref, bias_ref, mnew_ref, dnew_ref, cmla_ref, cdsa_ref,
                  olat_ref, odsa_ref,
                  buf_m, buf_d, sem_m, sem_d, m1, l1, acc1, m2, l2, acc2, tail_m, tail_d,
                  *, cp, nc, ncp, page, t_new, n_seq):
    b = pl.program_id(0)

    def copies_for(s, c, slot):
        mla = [pltpu.make_async_copy(cmla_ref.at[pt_ref[s, c * cp + p]],
                                     buf_m.at[slot, :, pl.ds(p * page, page)], sem_m.at[slot])
               for p in range(cp)]
        return mla + _page_copies(pt_ref, cdsa_ref, buf_d, sem_d, s, c, slot, cp, page, 2 * DSA_HEAD_DIM)

    @pl.when(b == 0)
    def _():
        _ring_prologue(ATTN_RING_DEPTH, nc, n_seq, copies_for)

    for m_scr, l_scr, acc_scr in ((m1, l1, acc1), (m2, l2, acc2)):
        m_scr[...] = jnp.full(m_scr.shape, NEG_INF, F32)
        l_scr[...] = jnp.zeros(l_scr.shape, F32)
        acc_scr[...] = jnp.zeros(acc_scr.shape, F32)

    q_lat = qm_ref[:, 0:KV_LORA]
    q_rope = qm_ref[:, KV_LORA:MLA_ROW]
    qd = qd_ref[...]

    def dsa_update(rows_d, token_bias):
        kd = rows_d[:, 0:DSA_HEAD_DIM].astype(BF16)
        vd = rows_d[:, DSA_HEAD_DIM:2 * DSA_HEAD_DIM].astype(BF16)
        bias = jnp.concatenate([jnp.broadcast_to(token_bias(t), (DSA_HEADS, kd.shape[0])) for t in range(t_new)],
                               axis=0)
        _online_softmax_step(_nt_dot(qd, kd) * DSA_SCALE + bias, vd, m2, l2, acc2)

    def chunk(ci, carry):
        slot = _ring_step(b, ci, nc, ATTN_RING_DEPTH, copies_for)
        c_t = buf_m[slot, 0:KV_LORA, :].astype(BF16)
        kr_t = buf_m[slot, KV_LORA:MLA_ROW, :].astype(BF16)
        s = (jnp.dot(q_lat, c_t, preferred_element_type=F32)
             + jnp.dot(q_rope, kr_t, preferred_element_type=F32)) * MLA_SCALE
        _online_softmax_step(s, c_t, m1, l1, acc1, v_transposed=True)
        low = ci < ncp
        b8 = bias_ref[jnp.where(low, ci, ci - ncp)]
        dsa_update(buf_d[slot], lambda t: jnp.where(low, b8[t:t + 1], b8[t + HALF_ROWS:t + HALF_ROWS + 1]))
        return carry

    lax.fori_loop(0, nc, chunk, 0)

    tail_m[...] = jnp.zeros(tail_m.shape, F32)
    tail_m[0:t_new, :] = mnew_ref[...]
    tail_d[...] = jnp.zeros(tail_d.shape, F32)
    tail_d[0:t_new, :] = dnew_ref[...]
    c_kv = tail_m[:, 0:KV_LORA].astype(BF16)
    s = (_nt_dot(q_lat, c_kv) + _nt_dot(q_rope, tail_m[:, KV_LORA:MLA_ROW].astype(BF16))) * MLA_SCALE
    tok = lax.broadcasted_iota(jnp.int32, s.shape, 0) // MLA_HEADS
    col = lax.broadcasted_iota(jnp.int32, s.shape, 1)
    _online_softmax_step(jnp.where((col <= tok) & (col < t_new), s, NEG_INF), c_kv, m1, l1, acc1)
    tile, rows = _fold_slot(nc, ncp)
    dsa_update(tail_d[...], lambda t: bias_ref[tile, rows.start + t:rows.start + t + 1, 0:LANES])

    olat_ref[...] = (acc1[...] / l1[...]).astype(BF16)
    odsa_ref[...] = (acc2[...] / l2[...]).astype(BF16)


def _sattn(page_table, q_mla, q_dsa, bias, mla_new, dsa_new, cache_mla_t, cache_dsa):
    bd, n_pages = page_table.shape
    page = cache_dsa.shape[1]
    assert cache_mla_t.shape[1:] == (MLA_ROW, page)
    t_new = mla_new.shape[1]
    rows = q_mla.shape[1]
    cp = _pages_per_chunk(n_pages)
    nc = n_pages // cp
    ncp = bias.shape[1]
    ck = cp * page
    blk = lambda a: pl.BlockSpec((None,) + a.shape[1:], lambda b, pt: (b,) + (0,) * (a.ndim - 1))
    grid_spec = pltpu.PrefetchScalarGridSpec(
        num_scalar_prefetch=1,
        grid=(bd,),
        in_specs=[blk(q_mla), blk(q_dsa), blk(bias), blk(mla_new), blk(dsa_new),
                  pl.BlockSpec(memory_space=pl.ANY), pl.BlockSpec(memory_space=pl.ANY)],
        out_specs=(pl.BlockSpec((None, rows, KV_LORA), lambda b, pt: (b, 0, 0)),
                   pl.BlockSpec((None, rows, DSA_HEAD_DIM), lambda b, pt: (b, 0, 0))),
        scratch_shapes=[pltpu.VMEM((ATTN_RING_DEPTH, MLA_ROW, ck), F32),
                        pltpu.VMEM((ATTN_RING_DEPTH, ck, 2 * DSA_HEAD_DIM), F32),
                        pltpu.SemaphoreType.DMA((ATTN_RING_DEPTH,)), pltpu.SemaphoreType.DMA((ATTN_RING_DEPTH,)),
                        pltpu.VMEM((rows, 1), F32), pltpu.VMEM((rows, 1), F32), pltpu.VMEM((rows, KV_LORA), F32),
                        pltpu.VMEM((rows, 1), F32), pltpu.VMEM((rows, 1), F32),
                        pltpu.VMEM((rows, DSA_HEAD_DIM), F32),
                        pltpu.VMEM((LANES, MLA_ROW), F32), pltpu.VMEM((LANES, 2 * DSA_HEAD_DIM), F32)])
    return pl.pallas_call(
        functools.partial(_sattn_kernel, cp=cp, nc=nc, ncp=ncp, page=page, t_new=t_new, n_seq=bd),
        out_shape=(jax.ShapeDtypeStruct((bd, rows, KV_LORA), BF16),
                   jax.ShapeDtypeStruct((bd, rows, DSA_HEAD_DIM), BF16)),
        grid_spec=grid_spec,
        compiler_params=_cparams(1),
        name="sample_attention",
    )(page_table, q_mla, q_dsa, bias, mla_new, dsa_new, cache_mla_t, cache_dsa)


ROW_TILE = 256
PROJ_TILE = 512
MLA_TILE = 512
DSA_TILE = 512
MLA_HEAD_GROUPS = 2


def _tile(n, target):
    t = min(n, target)
    assert n % t == 0
    return t


def _prompt_layer(x, mod, weights):
    batch, seq, d = x.shape
    shift, scale, gate = mod
    g_pre, g_post, w_in_p, g_q, w_uq_p, g_kv, w_ukv_bf, w_out_bf = weights
    m = batch * seq
    x2 = x.reshape(m, d)
    tr, tm = _tile(seq, ROW_TILE), _tile(seq, PROJ_TILE)
    pos = jnp.arange(seq)
    c128, s128 = _rope_tables(pos, DSA_HEAD_DIM // 2)
    c64, s64 = _rope_tables(pos, MLA_ROPE // 2)
    per_batch = lambda a: a[:, None, :]
    group = lambda i: i // (seq // tr)

    h = _prenorm(x2, g_pre, per_batch(scale), per_batch(shift), tr, group)
    cq, sz, qd, qi, mla_row, dsa_row, idx_row, misc = _inproj(
        h, w_in_p, (c128, s128, c64, s64), g_q, g_kv, tm, seq // tm, True)
    q_hm = _uq(cq, w_uq_p, c64, s64, tm, seq // tm, True)
    tq_mla = _tile(seq, MLA_TILE)
    k_hm, vt_hm = _kvup(mla_row, misc, w_ukv_bf, tq_mla)
    o_mla = _mla_prompt(q_hm, k_hm, vt_hm, batch, seq, tq_mla, MLA_HEAD_GROUPS)
    o_dsa = _dsa_prompt(qi, idx_row, misc, qd, dsa_row, batch, seq, _tile(seq, DSA_TILE))
    o_raw = _outproj(o_mla, o_dsa, sz, w_out_bf, tm, True)
    y = _finish(x2, o_raw, per_batch(gate), g_post, tr, group)
    unflat = lambda a: a.reshape(batch, seq, a.shape[-1])
    return unflat(y), unflat(mla_row), unflat(dsa_row), unflat(idx_row)


def _sample_layer(x, mod, weights, cache_mla, cache_dsa, cache_idx, page_table):
    bd, t_new, d = x.shape
    shift, scale, gate = mod
    g_pre, g_post, w_in_p, g_q, w_uq_p, g_kv, w_ukv_bf, w_out_bf = weights
    m = bd * t_new
    assert t_new <= HALF_ROWS and t_new * IDX_HEADS == LANES
    x2 = x.reshape(m, d)
    tr, tm = _tile(m, ROW_TILE), _tile(m, PROJ_TILE)
    past = page_table.shape[1] * cache_mla.shape[1]
    pos = past + jnp.arange(t_new)
    per_row = lambda tab: jnp.tile(tab, (bd, 1))
    c128, s128 = map(per_row, _rope_tables(pos, DSA_HEAD_DIM // 2))
    c64, s64 = map(per_row, _rope_tables(pos, MLA_ROPE // 2))
    per_token = lambda a: jnp.repeat(a, t_new, axis=0).reshape(m // tr, tr, d)
    group = lambda i: i

    h = _prenorm(x2, g_pre, per_token(scale), per_token(shift), tr, group)
    cq, sz, qd, qi, mla_new, dsa_new, idx_new, misc = _inproj(
        h, w_in_p, (c128, s128, c64, s64), g_q, g_kv, tm, m // tm, False)
    q_nope, q_rope = _uq(cq, w_uq_p, c64, s64, tm, m // tm, False)
    q_mla = _absorb(q_nope, q_rope, w_ukv_bf).reshape(bd, t_new * MLA_HEADS, Q_MLA_W)

    w_idx = misc[:, W_IDX_LANE0:W_IDX_LANE0 + IDX_HEADS].reshape(bd, 1, t_new, IDX_HEADS)
    wsel = (jnp.eye(t_new, dtype=F32)[None, :, :, None] * w_idx).reshape(bd, t_new, t_new * IDX_HEADS)
    wsel = jnp.pad(wsel, ((0, 0), (0, HALF_ROWS - t_new), (0, 0)))
    wsel = jnp.pad(jnp.concatenate([wsel, wsel], axis=1), ((0, 0), (0, SCORE_ROWS), (0, 0))).astype(BF16)

    topk = min(TOPK_MAX, (past + t_new) // 4)
    per_seq = lambda a, rows: a.reshape(bd, rows, -1)
    scores = _sidx(page_table, per_seq(qi, t_new * IDX_HEADS), wsel, per_seq(idx_new, t_new), cache_idx)
    bias = _sselect(scores, topk, t_new, n_keys=2 * scores.shape[1] * scores.shape[3])
    o_lat, o_dsa = _sattn(page_table, q_mla, per_seq(qd, t_new * DSA_HEADS), bias,
                          per_seq(mla_new, t_new), per_seq(dsa_new, t_new),
                          jnp.swapaxes(cache_mla, 1, 2), cache_dsa)
    o_mla = _unabsorb(o_lat.reshape(m, MLA_HEADS * KV_LORA), w_ukv_bf)
    o_raw = _outproj(o_mla, o_dsa.reshape(m, DSA_WIDTH), sz, w_out_bf, tm, False)
    y = _finish(x2, o_raw, per_token(gate), g_post, tr, group)
    unflat = lambda a: a.reshape(bd, t_new, a.shape[-1])
    return unflat(y), unflat(mla_new), unflat(dsa_new), unflat(idx_new)


def kernel(x_prompt, x_sample, cache_mla, cache_dsa_kv, cache_idx_k, page_table, c_prompt, c_sample,
           g_pre, g_post, w_ada, b_ada, w_in, g_q_lora, w_uq, g_kv_lora, w_ukv, w_out):
    depth = w_in.shape[0]
    batch, d = c_prompt.shape
    bd = c_sample.shape[0]
    yp, ys = x_prompt, x_sample
    new_rows = [[] for _ in range(6)]
    for l in range(depth):
        c_all = jnp.concatenate([c_prompt, c_sample], axis=0)
        c_all = jnp.pad(c_all, ((0, -c_all.shape[0] % 8), (0, 0)))
        ada = _ada(c_all, w_ada[l], b_ada[l][None])
        mods = [ada[:, k * d:(k + 1) * d] for k in range(3)]
        mod_p = [a[:batch] for a in mods]
        mod_s = [a[batch:batch + bd] for a in mods]
        row = lambda a: a[l][None]
        weights = (row(g_pre), row(g_post), _permute_w_in(w_in[l]), row(g_q_lora), _permute_w_uq(w_uq[l]),
                   row(g_kv_lora), w_ukv[l].astype(BF16), w_out[l].astype(BF16))
        yp, mla_p, dsa_p, idx_p = _prompt_layer(yp, mod_p, weights)
        ys, mla_s, dsa_s, idx_s = _sample_layer(ys, mod_s, weights, cache_mla[l], cache_dsa_kv[l],
                                                cache_idx_k[l], page_table)
        for acc, r in zip(new_rows, (mla_p, dsa_p, idx_p, mla_s, dsa_s, idx_s)):
            acc.append(r)
    return (yp, ys) + tuple(jnp.stack(r) for r in new_rows)
```
